```python
import jax, jax.numpy as jnp
from jax import lax
import numpy as np

D_MODEL = 1024
BATCH = 8
SEQ = 8192
DEPTH = 1
DEC_BATCH = 128
DEC_SEQ = 4
PAST_LEN = 8192
PAGE_SIZE = 128

MIX_WIDTH = D_MODEL
RG_WIDTH = MIX_WIDTH // 2
ATTN_WIDTH = MIX_WIDTH - RG_WIDTH
HEAD_DIM = 64
N_HEADS = ATTN_WIDTH // HEAD_DIM
RG_BLOCKS = 8
RG_BLOCK_W = RG_WIDTH // RG_BLOCKS
CONV_WIDTH = 4
RG_C = 8.0
FFN_HIDDEN = -(-8 * D_MODEL // (3 * 256)) * 256
IN_COLS = 2 * RG_WIDTH + 3 * ATTN_WIDTH + N_HEADS
Q_BLOCK = 128
EPS = 1e-6

kernel_name = "hymba_rglru_fox_adaln_decode_step"


def rms_norm(x, g):
    xf = x.astype(jnp.float32)
    y = xf * lax.rsqrt(jnp.mean(xf * xf, axis=-1, keepdims=True) + EPS)
    return (y * g.astype(jnp.float32)).astype(x.dtype)


def ada_mod(c, w_ada, b_ada):
    m = jax.nn.silu(c) @ w_ada + b_ada
    return jnp.split(m, 6, axis=-1)


def modulate(x, g, shift, scale):
    return rms_norm(x, g) * (1 + scale[:, None, :]) + shift[:, None, :]


def mixer_inputs(x, c, norm_mix, w_ada, b_ada, w_in, b_f, g_q, g_k):
    B, T = x.shape[:2]
    mods = ada_mod(c, w_ada, b_ada)
    xn = modulate(x, norm_mix, mods[0], mods[1])
    z = xn @ w_in
    cuts = [RG_WIDTH, 2 * RG_WIDTH, 2 * RG_WIDTH + ATTN_WIDTH,
            2 * RG_WIDTH + 2 * ATTN_WIDTH, 2 * RG_WIDTH + 3 * ATTN_WIDTH]
    xr, gr, q, k, v, fl = jnp.split(z, cuts, axis=-1)
    q = rms_norm(q.reshape(B, T, N_HEADS, HEAD_DIM), g_q)
    k = rms_norm(k.reshape(B, T, N_HEADS, HEAD_DIM), g_k)
    v = v.reshape(B, T, N_HEADS, HEAD_DIM)
    logf = jax.nn.log_sigmoid(fl.astype(jnp.float32) + b_f.astype(jnp.float32))
    return mods, xr, gr, q, k, v, logf


def causal_conv(x_ext, w, b, T):
    out = x_ext[:, 0:T] * w[0]
    for j in range(1, CONV_WIDTH):
        out = out + x_ext[:, j:j + T] * w[j]
    return out + b


def rg_branch(x_ext, gr, h0, w_conv, b_conv, w_a, b_a, w_x, b_x, lam):
    B, T = gr.shape[:2]
    xc = causal_conv(x_ext, w_conv, b_conv, T).astype(jnp.float32)
    xb = xc.reshape(B, T, RG_BLOCKS, RG_BLOCK_W)
    r = jax.nn.sigmoid(jnp.einsum('btnc,ncd->btnd', xb, w_a.astype(jnp.float32)).reshape(B, T, RG_WIDTH)
                       + b_a.astype(jnp.float32))
    i = jax.nn.sigmoid(jnp.einsum('btnc,ncd->btnd', xb, w_x.astype(jnp.float32)).reshape(B, T, RG_WIDTH)
                       + b_x.astype(jnp.float32))
    log_a = -RG_C * r * jax.nn.softplus(-lam.astype(jnp.float32))
    a = jnp.exp(log_a)
    u = jnp.sqrt(-jnp.expm1(2.0 * log_a)) * (i * xc)

    def step(h, au):
        h = au[0] * h + au[1]
        return h, h

    hT, hs = lax.scan(step, h0, (a.swapaxes(0, 1), u.swapaxes(0, 1)))
    out = hs.swapaxes(0, 1).astype(gr.dtype) * jax.nn.gelu(gr)
    return out, hT


def fox_prompt(q, k, v, logf):
    B, S = q.shape[:2]
    scale = HEAD_DIM ** -0.5
    Fk = jnp.cumsum(logf, axis=1).transpose(0, 2, 1)
    kpos = jnp.arange(S)

    def block(i):
        start = i * Q_BLOCK
        qb = lax.dynamic_slice_in_dim(q, start, Q_BLOCK, axis=1)
        Fq = lax.dynamic_slice_in_dim(Fk, start, Q_BLOCK, axis=2)
        s = jnp.einsum('bqhd,bkhd->bhqk', qb, k, preferred_element_type=jnp.float32) * scale
        s = s + Fq[..., None] - Fk[:, :, None, :]
        qpos = start + jnp.arange(Q_BLOCK)
        s = jnp.where(kpos[None, :] <= qpos[:, None], s, -jnp.inf)
        p = jax.nn.softmax(s, axis=-1)
        return jnp.einsum('bhqk,bkhd->bqhd', p.astype(v.dtype), v)

    out = lax.map(block, jnp.arange(S // Q_BLOCK))
    return out.transpose(1, 0, 2, 3, 4).reshape(B, S, ATTN_WIDTH)


def fox_sample(q, k, v, logf, cache_k, cache_v, cache_logf, page_table):
    B, T = q.shape[:2]
    n_pages = page_table.shape[1]
    past = n_pages * PAGE_SIZE
    L = past + T
    scale = HEAD_DIM ** -0.5
    mask = jnp.arange(L)[None, :] <= (past + jnp.arange(T))[:, None]

    def one(args):
        qs, ks, vs, lfs, pages = args
        kp = cache_k[pages].reshape(past, N_HEADS, HEAD_DIM)
        vp = cache_v[pages].reshape(past, N_HEADS, HEAD_DIM)
        lfp = cache_logf[pages].reshape(past, N_HEADS).astype(jnp.float32)
        k_all = jnp.concatenate([kp, ks.astype(kp.dtype)], axis=0)
        v_all = jnp.concatenate([vp, vs.astype(vp.dtype)], axis=0)
        F = jnp.cumsum(jnp.concatenate([lfp, lfs], axis=0), axis=0)
        s = jnp.einsum('thd,lhd->htl', qs, k_all, preferred_element_type=jnp.float32) * scale
        s = s + F[past:].T[:, :, None] - F.T[:, None, :]
        s = jnp.where(mask[None], s, -jnp.inf)
        p = jax.nn.softmax(s, axis=-1)
        return jnp.einsum('htl,lhd->thd', p.astype(v_all.dtype), v_all)

    out = lax.map(one, (q, k, v, logf, page_table))
    return out.reshape(B, T, ATTN_WIDTH)


def merge_and_ffn(x, mods, rg_out, attn_out, norm_rg, norm_attn, w_out, norm_ffn, w_fg, w_fu, w_fd):
    mix = jnp.concatenate([rms_norm(rg_out, norm_rg), rms_norm(attn_out.astype(x.dtype), norm_attn)], axis=-1) @ w_out
    x = x + (1 + mods[2])[:, None, :] * mix
    xn = modulate(x, norm_ffn, mods[3], mods[4])
    f = (jax.nn.silu(xn @ w_fg) * (xn @ w_fu)) @ w_fd
    return x + (1 + mods[5])[:, None, :] * f


def setup_inputs(seed: int = 0) -> dict:
    key = jax.random.key(seed)
    ks = iter(jax.random.split(key, 48))

    def nrm(shape, s):
        return jax.random.normal(next(ks), shape, jnp.float32) * s

    n_pages = PAST_LEN // PAGE_SIZE
    n_used = DEC_BATCH * n_pages
    n_pool = (5 * n_used) // 4
    page_table = jax.random.permutation(next(ks), n_pool)[:n_used].reshape(DEC_BATCH, n_pages).astype(jnp.int32)
    u = jax.random.uniform(next(ks), (DEPTH, RG_WIDTH), jnp.float32, 0.9, 0.999)
    base = u ** (1.0 / RG_C)
    lam = jnp.log(base) - jnp.log1p(-base)
    b_f = 3.0 + 3.0 * jax.random.uniform(next(ks), (DEPTH, N_HEADS), jnp.float32)
    d = D_MODEL
    return {
        "x_prompt": nrm((BATCH, SEQ, d), 1.0),
        "x_sample": nrm((DEC_BATCH, DEC_SEQ, d), 1.0),
        "cache_k": nrm((DEPTH, n_pool, PAGE_SIZE, N_HEADS, HEAD_DIM), 1.0),
        "cache_v": nrm((DEPTH, n_pool, PAGE_SIZE, N_HEADS, HEAD_DIM), 1.0),
        "cache_logf": jax.nn.log_sigmoid(4.0 + nrm((DEPTH, n_pool, PAGE_SIZE, N_HEADS), 0.5)),
        "state_conv": nrm((DEPTH, DEC_BATCH, CONV_WIDTH - 1, RG_WIDTH), 1.0),
        "state_h": nrm((DEPTH, DEC_BATCH, RG_WIDTH), 0.5),
        "page_table": page_table,
        "c_prompt": nrm((BATCH, d), 1.0),
        "c_sample": nrm((DEC_BATCH, d), 1.0),
        "norm_mix": 1.0 + nrm((DEPTH, d), 0.02),
        "w_ada": nrm((DEPTH, d, 6 * d), 0.1 * d ** -0.5),
        "b_ada": nrm((DEPTH, 6 * d), 0.01),
        "w_in": nrm((DEPTH, d, IN_COLS), d ** -0.5),
        "b_f": b_f,
        "g_q": 1.0 + nrm((DEPTH, HEAD_DIM), 0.02),
        "g_k": 1.0 + nrm((DEPTH, HEAD_DIM), 0.02),
        "w_conv": nrm((DEPTH, CONV_WIDTH, RG_WIDTH), CONV_WIDTH ** -0.5),
        "b_conv": nrm((DEPTH, RG_WIDTH), 0.01),
        "w_a": nrm((DEPTH, RG_BLOCKS, RG_BLOCK_W, RG_BLOCK_W), RG_BLOCK_W ** -0.5),
        "b_a": nrm((DEPTH, RG_WIDTH), 0.01),
        "w_x": nrm((DEPTH, RG_BLOCKS, RG_BLOCK_W, RG_BLOCK_W), RG_BLOCK_W ** -0.5),
        "b_x": nrm((DEPTH, RG_WIDTH), 0.01),
        "lam": lam,
        "norm_rg": 1.0 + nrm((DEPTH, RG_WIDTH), 0.02),
        "norm_attn": 1.0 + nrm((DEPTH, ATTN_WIDTH), 0.02),
        "w_out": nrm((DEPTH, MIX_WIDTH, d), MIX_WIDTH ** -0.5),
        "norm_ffn": 1.0 + nrm((DEPTH, d), 0.02),
        "w_ffn_gate": nrm((DEPTH, d, FFN_HIDDEN), d ** -0.5),
        "w_ffn_up": nrm((DEPTH, d, FFN_HIDDEN), d ** -0.5),
        "w_ffn_down": nrm((DEPTH, FFN_HIDDEN, d), FFN_HIDDEN ** -0.5),
    }


def reference(x_prompt, x_sample, cache_k, cache_v, cache_logf, state_conv, state_h, page_table,
              c_prompt, c_sample, norm_mix, w_ada, b_ada, w_in, b_f, g_q, g_k, w_conv, b_conv,
              w_a, b_a, w_x, b_x, lam, norm_rg, norm_attn, w_out, norm_ffn,
              w_ffn_gate, w_ffn_up, w_ffn_down):
    xp, xs = x_prompt, x_sample
    kp_l, vp_l, lfp_l, cp_l, hp_l = [], [], [], [], []
    ks_l, vs_l, lfs_l, cs_l, hs_l = [], [], [], [], []
    for l in range(DEPTH):
        mods, xr, gr, q, k, v, logf = mixer_inputs(xp, c_prompt, norm_mix[l], w_ada[l], b_ada[l],
                                                   w_in[l], b_f[l], g_q[l], g_k[l])
        x_ext = jnp.pad(xr, ((0, 0), (CONV_WIDTH - 1, 0), (0, 0)))
        h0 = jnp.zeros((xp.shape[0], RG_WIDTH), jnp.float32)
        rg_out, hT = rg_branch(x_ext, gr, h0, w_conv[l], b_conv[l], w_a[l], b_a[l], w_x[l], b_x[l], lam[l])
        attn_out = fox_prompt(q, k, v, logf)
        xp = merge_and_ffn(xp, mods, rg_out, attn_out, norm_rg[l], norm_attn[l], w_out[l],
                           norm_ffn[l], w_ffn_gate[l], w_ffn_up[l], w_ffn_down[l])
        kp_l.append(k); vp_l.append(v); lfp_l.append(logf)
        cp_l.append(x_ext[:, -(CONV_WIDTH - 1):]); hp_l.append(hT)

        mods, xr, gr, q, k, v, logf = mixer_inputs(xs, c_sample, norm_mix[l], w_ada[l], b_ada[l],
                                                   w_in[l], b_f[l], g_q[l], g_k[l])
        x_ext = jnp.concatenate([state_conv[l].astype(xr.dtype), xr], axis=1)
        h0 = state_h[l].astype(jnp.float32)
        rg_out, hT = rg_branch(x_ext, gr, h0, w_conv[l], b_conv[l], w_a[l], b_a[l], w_x[l], b_x[l], lam[l])
        attn_out = fox_sample(q, k, v, logf, cache_k[l], cache_v[l], cache_logf[l], page_table)
        xs = merge_and_ffn(xs, mods, rg_out, attn_out, norm_rg[l], norm_attn[l], w_out[l],
                           norm_ffn[l], w_ffn_gate[l], w_ffn_up[l], w_ffn_down[l])
        ks_l.append(k); vs_l.append(v); lfs_l.append(logf)
        cs_l.append(x_ext[:, -(CONV_WIDTH - 1):]); hs_l.append(hT)

    return (xp, xs,
            jnp.stack(kp_l), jnp.stack(vp_l), jnp.stack(lfp_l), jnp.stack(cp_l), jnp.stack(hp_l),
            jnp.stack(ks_l), jnp.stack(vs_l), jnp.stack(lfs_l), jnp.stack(cs_l), jnp.stack(hs_l))
```

```python
import functools

import jax
import jax.numpy as jnp
from jax import lax
from jax.experimental import pallas as pl
from jax.experimental.pallas import tpu as pltpu

EPS = 1e-6
RG_C = 8.0
CONV_WIDTH = 4
LANES = 128
SUBLANES = 8
NEG_BIG = -1e30
VMEM_LIMIT = 56 * 1024 * 1024

F32 = jnp.float32
BF16 = jnp.bfloat16


def _cparams(sem, vmem=VMEM_LIMIT):
    return pltpu.CompilerParams(dimension_semantics=sem, vmem_limit_bytes=vmem)


def _rms_rows(x):
    return x * lax.rsqrt(jnp.mean(x * x, axis=-1, keepdims=True) + EPS)


def _log_sigmoid(x):
    return -(jnp.maximum(-x, 0.0) + jnp.log1p(jnp.exp(-jnp.abs(x))))


def _cumsum_rows(x, seg=None):
    n = x.shape[0]
    seg = n if seg is None else seg
    row = lax.broadcasted_iota(jnp.int32, x.shape, 0)
    pos = row if seg == n else jnp.bitwise_and(row, seg - 1)
    d = 1
    while d < seg:
        x = x + jnp.where(pos >= d, pltpu.roll(x, d, axis=0), 0.0)
        d *= 2
    return x


def _scan_rows(a, u, seg=None):
    n = a.shape[0]
    seg = n if seg is None else seg
    row = lax.broadcasted_iota(jnp.int32, a.shape, 0)
    pos = row if seg == n else jnp.bitwise_and(row, seg - 1)
    d = 1
    while d < seg:
        m = pos >= d
        a_s = jnp.where(m, pltpu.roll(a, d, axis=0), 1.0)
        u_s = jnp.where(m, pltpu.roll(u, d, axis=0), 0.0)
        u = u + a * u_s
        a = a * a_s
        d *= 2
    return a, u


def _ada_kernel(c_ref, w_ref, b_ref, o_ref):
    c = c_ref[...]
    s = (c * jax.nn.sigmoid(c)).astype(BF16)
    o_ref[...] = jnp.dot(s, w_ref[...].astype(BF16), preferred_element_type=F32) + b_ref[...]


def _ada(c, w, b):
    m, d = c.shape
    n = w.shape[1]
    tn = n // 4
    return pl.pallas_call(
        _ada_kernel,
        out_shape=jax.ShapeDtypeStruct((m, n), F32),
        grid=(n // tn,),
        in_specs=[pl.BlockSpec((m, d), lambda j: (0, 0)),
                  pl.BlockSpec((d, tn), lambda j: (0, j)),
                  pl.BlockSpec((1, tn), lambda j: (0, j))],
        out_specs=pl.BlockSpec((m, tn), lambda j: (0, j)),
        compiler_params=_cparams(("arbitrary",)),
        name="ada_mod",
    )(c, w, b)


def _head_rms(t, ones_bf16, gain_row, hd):
    ssq = jnp.dot((t * t).astype(BF16), ones_bf16, preferred_element_type=F32)
    return t * lax.rsqrt(ssq * (1.0 / hd) + EPS) * gain_row


def _inproj_body(x_ref, shift_ref, scale_ref, g_ref, w_ref, ones_ref, gq_ref, gk_ref, bf_ref, rg, att, hd, nh):
    x = x_ref[...]
    xn = _rms_rows(x) * g_ref[...]
    xn = xn * (1.0 + scale_ref[0]) + shift_ref[0]
    z = jnp.dot(xn.astype(BF16), w_ref[...], preferred_element_type=F32)
    xr = z[:, 0:rg]
    gr = z[:, rg:2 * rg]
    q = z[:, 2 * rg:2 * rg + att]
    k = z[:, 2 * rg + att:2 * rg + 2 * att]
    v = z[:, 2 * rg + 2 * att:2 * rg + 3 * att]
    fl = z[:, 2 * rg + 3 * att:2 * rg + 3 * att + nh]
    ones = ones_ref[...]
    qn = _head_rms(q, ones, gq_ref[...], hd)
    kn = _head_rms(k, ones, gk_ref[...], hd)
    logf = _log_sigmoid(fl + bf_ref[...])
    return xr, gr, qn, kn, v, logf


def _inproj_prompt_kernel(x_ref, shift_ref, scale_ref, g_ref, w_ref, ones_ref, gq_ref, gk_ref, bf_ref,
                          xr_ref, gr_ref, kn_ref, kb_ref, v_ref, qT_ref, vT_ref, logf_ref, fc_ref,
                          carry_ref, *, rg, att, hd, nh):
    t = pl.program_id(1)

    @pl.when(t == 0)
    def _():
        carry_ref[...] = jnp.zeros_like(carry_ref)

    xr, gr, qn, kn, v, logf = _inproj_body(x_ref, shift_ref, scale_ref, g_ref, w_ref, ones_ref,
                                           gq_ref, gk_ref, bf_ref, rg, att, hd, nh)
    xr_ref[...] = xr
    gr_ref[...] = gr
    kn_ref[...] = kn
    kb_ref[...] = kn.astype(BF16)
    v_ref[...] = v
    qT_ref[0] = qn.T.astype(BF16)
    vT_ref[0] = v.T.astype(BF16)
    logf_ref[...] = logf
    fc = _cumsum_rows(logf) + carry_ref[0:1, 0:nh]
    fc_ref[...] = fc
    n = fc.shape[0]
    carry_ref[0:1, 0:nh] = fc[n - 1:n, :]


def _inproj_sample_kernel(x_ref, shift_ref, scale_ref, g_ref, w_ref, ones_ref, gq_ref, gk_ref, bf_ref,
                          xr_ref, gr_ref, q_ref, kn_ref, v_ref, logf_ref, *, rg, att, hd, nh):
    xr, gr, qn, kn, v, logf = _inproj_body(x_ref, shift_ref, scale_ref, g_ref, w_ref, ones_ref,
                                           gq_ref, gk_ref, bf_ref, rg, att, hd, nh)
    xr_ref[...] = xr
    gr_ref[...] = gr
    q_ref[...] = qn
    kn_ref[...] = kn
    v_ref[...] = v
    logf_ref[...] = logf


def _inproj_common_specs(d, wcols, att, nh, mod_rows, mod_map):
    const = lambda *_: (0, 0)
    return [pl.BlockSpec((1, mod_rows, d), mod_map),
            pl.BlockSpec((1, mod_rows, d), mod_map),
            pl.BlockSpec((1, d), const),
            pl.BlockSpec((d, wcols), const),
            pl.BlockSpec((att, att), const),
            pl.BlockSpec((1, att), const),
            pl.BlockSpec((1, att), const),
            pl.BlockSpec((1, nh), const)]


def _inproj_prompt(x2d, shift, scale, g, w, ones, gq, gk, bf, *, b, t, tm, rg, att, hd, nh):
    n, d = x2d.shape
    nt = t // tm
    row = lambda bi, ti: (bi * nt + ti, 0)
    tr = lambda bi, ti: (bi, 0, ti)
    kern = functools.partial(_inproj_prompt_kernel, rg=rg, att=att, hd=hd, nh=nh)
    return pl.pallas_call(
        kern,
        out_shape=(jax.ShapeDtypeStruct((n, rg), F32), jax.ShapeDtypeStruct((n, rg), F32),
                   jax.ShapeDtypeStruct((n, att), F32), jax.ShapeDtypeStruct((n, att), BF16),
                   jax.ShapeDtypeStruct((n, att), F32),
                   jax.ShapeDtypeStruct((b, att, t), BF16), jax.ShapeDtypeStruct((b, att, t), BF16),
                   jax.ShapeDtypeStruct((n, nh), F32), jax.ShapeDtypeStruct((n, nh), F32)),
        grid=(b, nt),
        in_specs=[pl.BlockSpec((tm, d), row)] + _inproj_common_specs(
            d, w.shape[1], att, nh, 1, lambda bi, ti: (bi, 0, 0)),
        out_specs=(pl.BlockSpec((tm, rg), row), pl.BlockSpec((tm, rg), row),
                   pl.BlockSpec((tm, att), row), pl.BlockSpec((tm, att), row),
                   pl.BlockSpec((tm, att), row),
                   pl.BlockSpec((1, att, tm), tr), pl.BlockSpec((1, att, tm), tr),
                   pl.BlockSpec((tm, nh), row), pl.BlockSpec((tm, nh), row)),
        scratch_shapes=[pltpu.VMEM((SUBLANES, LANES), F32)],
        compiler_params=_cparams(("arbitrary", "arbitrary")),
        name="inproj_prompt",
    )(x2d, shift, scale, g, w, ones, gq, gk, bf)


def _inproj_sample(x2d, shift, scale, g, w, ones, gq, gk, bf, *, rg, att, hd, nh):
    n, d = x2d.shape
    row = lambda i: (0, 0)
    kern = functools.partial(_inproj_sample_kernel, rg=rg, att=att, hd=hd, nh=nh)
    return pl.pallas_call(
        kern,
        out_shape=(jax.ShapeDtypeStruct((n, rg), F32), jax.ShapeDtypeStruct((n, rg), F32),
                   jax.ShapeDtypeStruct((n, att), F32), jax.ShapeDtypeStruct((n, att), F32),
                   jax.ShapeDtypeStruct((n, att), F32), jax.ShapeDtypeStruct((n, nh), F32)),
        grid=(1,),
        in_specs=[pl.BlockSpec((n, d), row)] + _inproj_common_specs(
            d, w.shape[1], att, nh, n, lambda i: (0, 0, 0)),
        out_specs=(pl.BlockSpec((n, rg), row), pl.BlockSpec((n, rg), row),
                   pl.BlockSpec((n, att), row), pl.BlockSpec((n, att), row),
                   pl.BlockSpec((n, att), row), pl.BlockSpec((n, nh), row)),
        compiler_params=_cparams(("arbitrary",)),
        name="inproj_sample",
    )(x2d, shift, scale, g, w, ones, gq, gk, bf)


def _rg_gates(xc, wa_ref, ba, bx, lam):
    sp = jnp.maximum(-lam, 0.0) + jnp.log1p(jnp.exp(-jnp.abs(lam)))
    a_parts, u_parts = [], []
    npair = wa_ref.shape[0]
    pw = wa_ref.shape[1]
    for j in range(npair):
        sl = slice(pw * j, pw * (j + 1))
        xcj = xc[:, sl]
        g = jnp.dot(xcj.astype(BF16), wa_ref[j], preferred_element_type=F32)
        r = jax.nn.sigmoid(g[:, :pw] + ba[:, sl])
        i = jax.nn.sigmoid(g[:, pw:] + bx[:, sl])
        log_a = (-RG_C) * r * sp[:, sl]
        a = jnp.exp(log_a)
        a_parts.append(a)
        u_parts.append(jnp.sqrt(-jnp.tanh(log_a) * (a * a + 1.0)) * (i * xcj))
    return a_parts, u_parts


def _rg_finish(h, gr, norm_row):
    out = h * jax.nn.gelu(gr)
    return (_rms_rows(out) * norm_row).astype(BF16)


def _rg_prompt_kernel(xr_ref, gr_ref, wc_ref, bc_ref, wa_ref, ba_ref, bx_ref, lam_ref, nrm_ref,
                      o_ref, hT_ref, xbuf_ref, h_ref, *, tm):
    t = pl.program_id(1)

    @pl.when(t == 0)
    def _():
        xbuf_ref[0:SUBLANES, :] = jnp.zeros((SUBLANES, xbuf_ref.shape[1]), F32)
        h_ref[...] = jnp.zeros_like(h_ref)

    xbuf_ref[SUBLANES:SUBLANES + tm, :] = xr_ref[...]
    xc = bc_ref[...]
    for j in range(CONV_WIDTH):
        off = SUBLANES - (CONV_WIDTH - 1) + j
        xc = xc + xbuf_ref[off:off + tm, :] * wc_ref[j:j + 1, :]
    xbuf_ref[0:SUBLANES, :] = xbuf_ref[tm:tm + SUBLANES, :]

    a_parts, u_parts = _rg_gates(xc, wa_ref, ba_ref[...], bx_ref[...], lam_ref[...])
    pw = wa_ref.shape[1]
    h_parts = []
    for j, (a, u) in enumerate(zip(a_parts, u_parts)):
        big_a, big_u = _scan_rows(a, u)
        h_parts.append(big_a * h_ref[0:1, pw * j:pw * (j + 1)] + big_u)
    h = jnp.concatenate(h_parts, axis=1)
    h_last = h[tm - 1:tm, :]
    h_ref[0:1, :] = h_last
    hT_ref[0] = h_last
    o_ref[...] = _rg_finish(h, gr_ref[...], nrm_ref[...])


def _rg_prompt(xr, gr, wc, bc, wa, ba, bx, lam, nrm, *, b, t, tm):
    n, rg = xr.shape
    nt = t // tm
    row = lambda bi, ti: (bi * nt + ti, 0)
    const2 = lambda bi, ti: (0, 0)
    return pl.pallas_call(
        functools.partial(_rg_prompt_kernel, tm=tm),
        out_shape=(jax.ShapeDtypeStruct((n, rg), BF16), jax.ShapeDtypeStruct((b, 1, rg), F32)),
        grid=(b, nt),
        in_specs=[pl.BlockSpec((tm, rg), row), pl.BlockSpec((tm, rg), row),
                  pl.BlockSpec(wc.shape, const2), pl.BlockSpec((1, rg), const2),
                  pl.BlockSpec(wa.shape, lambda bi, ti: (0, 0, 0)),
                  pl.BlockSpec((1, rg), const2), pl.BlockSpec((1, rg), const2),
                  pl.BlockSpec((1, rg), const2), pl.BlockSpec((1, rg), const2)],
        out_specs=(pl.BlockSpec((tm, rg), row), pl.BlockSpec((1, 1, rg), lambda bi, ti: (bi, 0, 0))),
        scratch_shapes=[pltpu.VMEM((tm + SUBLANES, rg), F32), pltpu.VMEM((SUBLANES, rg), F32)],
        compiler_params=_cparams(("arbitrary", "arbitrary")),
        name="rg_prompt",
    )(xr, gr, wc, bc, wa, ba, bx, lam, nrm)


def _rg_sample_kernel(x0_ref, x1_ref, x2_ref, x3_ref, gr_ref, h0_ref, wc_ref, bc_ref, wa_ref, ba_ref, bx_ref,
                      lam_ref, nrm_ref, o_ref, h_ref, *, seg):
    xs = (x0_ref, x1_ref, x2_ref, x3_ref)
    xc = bc_ref[...]
    for j in range(CONV_WIDTH):
        xc = xc + xs[j][...] * wc_ref[j:j + 1, :]
    a_parts, u_parts = _rg_gates(xc, wa_ref, ba_ref[...], bx_ref[...], lam_ref[...])
    pw = wa_ref.shape[1]
    h_parts = []
    for j, (a, u) in enumerate(zip(a_parts, u_parts)):
        big_a, big_u = _scan_rows(a, u, seg=seg)
        h_parts.append(big_a * h0_ref[:, pw * j:pw * (j + 1)] + big_u)
    h = jnp.concatenate(h_parts, axis=1)
    h_ref[...] = h
    o_ref[...] = _rg_finish(h, gr_ref[...], nrm_ref[...])


def _rg_sample(xwin, gr, h0_rows, wc, bc, wa, ba, bx, lam, nrm, *, seg):
    n, rg = gr.shape
    full = pl.BlockSpec((n, rg), lambda i: (0, 0))
    vec = pl.BlockSpec((1, rg), lambda i: (0, 0))
    return pl.pallas_call(
        functools.partial(_rg_sample_kernel, seg=seg),
        out_shape=(jax.ShapeDtypeStruct((n, rg), BF16), jax.ShapeDtypeStruct((n, rg), F32)),
        grid=(1,),
        in_specs=[full, full, full, full, full, full,
                  pl.BlockSpec(wc.shape, lambda i: (0, 0)), vec,
                  pl.BlockSpec(wa.shape, lambda i: (0, 0, 0)), vec, vec, vec, vec],
        out_specs=(full, full),
        compiler_params=_cparams(("arbitrary",)),
        name="rg_sample",
    )(*xwin, gr, h0_rows, wc, bc, wa, ba, bx, lam, nrm)


def _attn_prompt_kernel(qT_ref, kb_ref, vT_ref, fc_ref, o_ref, fcol_ref, *, tq, hd, nh):
    hp = pl.program_id(1)
    i = pl.program_id(2)
    t_all = kb_ref.shape[1]

    @pl.when(i == 0)
    def _():
        fc = fc_ref[0]
        lane = lax.broadcasted_iota(jnp.int32, fc.shape, 1)
        for hh in range(2):
            col = jnp.sum(jnp.where(lane == 2 * hp + hh, fc, 0.0), axis=1, keepdims=True)
            fcol_ref[hh] = jnp.broadcast_to(-col, (t_all, LANES))

    qp = qT_ref[0]
    rowq = lax.broadcasted_iota(jnp.int32, qp.shape, 0)
    zero = jnp.zeros_like(qp)
    qs = (jnp.where(rowq < hd, qp, zero), jnp.where(rowq >= hd, qp, zero))
    rep = tq // LANES

    def block(j, carry, masked):
        start = pl.multiple_of(j * tq, tq)
        kblk = kb_ref[0, pl.ds(start, tq), :]
        new = []
        for hh in range(2):
            m, l, acc = carry[hh]
            s = jnp.dot(kblk, qs[hh], preferred_element_type=F32)
            bias = fcol_ref[hh, pl.ds(start, tq), :]
            s = s + jnp.concatenate([bias] * rep, axis=1)
            if masked:
                rk = lax.broadcasted_iota(jnp.int32, s.shape, 0)
                cq = lax.broadcasted_iota(jnp.int32, s.shape, 1)
                s = jnp.where(rk <= cq, s, NEG_BIG)
            m_new = jnp.maximum(m, jnp.max(s, axis=0, keepdims=True))
            p = jnp.exp(s - m_new)
            alpha = jnp.exp(m - m_new)
            l_new = alpha * l + jnp.sum(p, axis=0, keepdims=True)
            vblk = vT_ref[0, hh * hd:(hh + 1) * hd, pl.ds(start, tq)]
            pv = jnp.dot(vblk, p.astype(BF16), preferred_element_type=F32)
            new.append((m_new, l_new, alpha * acc + pv))
        return tuple(new)

    init = tuple((jnp.full((1, tq), NEG_BIG, F32), jnp.zeros((1, tq), F32), jnp.zeros((hd, tq), F32))
                 for _ in range(2))
    carry = lax.fori_loop(0, i, lambda j, c: block(j, c, False), init)
    carry = block(i, carry, True)
    oT = jnp.concatenate([carry[hh][2] / carry[hh][1] for hh in range(2)], axis=0)
    o_ref[0] = oT.T


def _attn_prompt(qT, kb, vT, fc, *, tq, hd, nh):
    b, att, t = qT.shape
    pair = 2 * hd
    return pl.pallas_call(
        functools.partial(_attn_prompt_kernel, tq=tq, hd=hd, nh=nh),
        out_shape=jax.ShapeDtypeStruct((b, t, att), F32),
        grid=(b, att // pair, t // tq),
        in_specs=[pl.BlockSpec((1, pair, tq), lambda bi, hp, i: (bi, hp, i)),
                  pl.BlockSpec((1, t, pair), lambda bi, hp, i: (bi, 0, hp)),
                  pl.BlockSpec((1, pair, t), lambda bi, hp, i: (bi, hp, 0)),
                  pl.BlockSpec((1, t, nh), lambda bi, hp, i: (bi, 0, 0))],
        out_specs=pl.BlockSpec((1, tq, pair), lambda bi, hp, i: (bi, i, hp)),
        scratch_shapes=[pltpu.VMEM((2, t, LANES), F32)],
        compiler_params=_cparams(("arbitrary", "arbitrary", "arbitrary")),
        name="attn_prompt",
    )(qT, kb, vT, fc)


def _row_to_cols(r, width):
    sq = jnp.broadcast_to(r, (LANES, LANES)).T
    return jnp.concatenate([sq] * (width // LANES), axis=1)


def _attn_sample_kernel(pt_ref, *refs, npg, ts, hd, nh):
    k_refs = refs[0:npg]
    v_refs = refs[npg:2 * npg]
    lf_refs = refs[2 * npg:3 * npg]
    q_ref, ks_ref, vs_ref, lfs_ref, o_ref, qp_ref, m_ref, l_ref, acc_ref, fcar_ref = refs[3 * npg:]
    g = pl.program_id(1)
    ng = pl.num_programs(1)
    att = nh * hd
    nrow = ts * nh

    @pl.when(g == 0)
    def _():
        q = q_ref[0]
        col_head = lax.broadcasted_iota(jnp.int32, (nh, att), 1) // hd
        row_head = lax.broadcasted_iota(jnp.int32, (nh, att), 0)
        blocks = [jnp.where(col_head == row_head, jnp.broadcast_to(q[tt:tt + 1, :], (nh, att)), 0.0)
                  for tt in range(ts)]
        blocks.append(jnp.zeros((LANES - nrow, att), F32))
        qp_ref[...] = jnp.concatenate(blocks, axis=0).astype(BF16)
        m_ref[...] = jnp.full(m_ref.shape, NEG_BIG, F32)
        l_ref[...] = jnp.zeros_like(l_ref)
        acc_ref[...] = jnp.zeros_like(acc_ref)
        fcar_ref[...] = jnp.zeros_like(fcar_ref)

    def update(kb, vb, bias, valid):
        sT = lax.dot_general(kb, qp_ref[...], (((1,), (1,)), ((), ())), preferred_element_type=F32)
        bias_t = jnp.concatenate([bias] * ts + [jnp.zeros((bias.shape[0], LANES - nrow), F32)], axis=1)
        sT = sT + bias_t
        if valid is not None:
            sT = jnp.where(valid, sT, NEG_BIG)
        m_old = m_ref[0:1, :]
        m_new = jnp.maximum(m_old, jnp.max(sT, axis=0, keepdims=True))
        pT = jnp.exp(sT - m_new)
        alpha = jnp.exp(m_old - m_new)
        l_ref[0:1, :] = alpha * l_ref[0:1, :] + jnp.sum(pT, axis=0, keepdims=True)
        m_ref[0:1, :] = m_new
        p = pT.T.astype(BF16)
        pv = jnp.dot(p, vb, preferred_element_type=F32)
        acc_ref[...] = acc_ref[...] * _row_to_cols(alpha, att) + pv

    kb = jnp.concatenate([r[0] for r in k_refs], axis=0).astype(BF16)
    vb = jnp.concatenate([r[0] for r in v_refs], axis=0).astype(BF16)
    lf = jnp.concatenate([r[0] for r in lf_refs], axis=0)
    fc = _cumsum_rows(lf) + fcar_ref[0:1, 0:nh]
    n = fc.shape[0]
    fcar_ref[0:1, 0:nh] = fc[n - 1:n, :]
    update(kb, vb, -fc, None)

    @pl.when(g == ng - 1)
    def _():
        npad = ks_ref.shape[1]
        fnew = _cumsum_rows(lfs_ref[0]) + fcar_ref[0:1, 0:nh]
        key_t = lax.broadcasted_iota(jnp.int32, (npad, LANES), 0)
        col = lax.broadcasted_iota(jnp.int32, (npad, LANES), 1)
        valid = (key_t <= col // nh) & (key_t < ts)
        update(ks_ref[0].astype(BF16), vs_ref[0].astype(BF16), -fnew, valid)
        o_all = acc_ref[...] / _row_to_cols(l_ref[0:1, :], att)
        r_head = jnp.bitwise_and(lax.broadcasted_iota(jnp.int32, (LANES, att), 0), nh - 1)
        c_head = lax.broadcasted_iota(jnp.int32, (LANES, att), 1) // hd
        o_sel = jnp.where(r_head == c_head, o_all, 0.0)[0:nrow, :]
        o_ref[0] = jnp.sum(o_sel.reshape(ts, nh, att), axis=1)


def _attn_sample(page_table, ck, cv, clf, q, ks, vs, lfs, *, npg, hd, nh):
    bs, ts, att = q.shape
    page = ck.shape[1]
    n_pages = page_table.shape[1]
    ngroups = n_pages // npg
    tpad = ks.shape[1]

    def page_map(ii):
        return lambda b, g, pt: (pt[b, g * npg + ii], 0, 0)

    per_b = lambda b, g, pt: (b, 0, 0)
    in_specs = ([pl.BlockSpec((1, page, att), page_map(ii)) for ii in range(npg)]
                + [pl.BlockSpec((1, page, att), page_map(ii)) for ii in range(npg)]
                + [pl.BlockSpec((1, page, nh), page_map(ii)) for ii in range(npg)]
                + [pl.BlockSpec((1, ts, att), per_b), pl.BlockSpec((1, tpad, att), per_b),
                   pl.BlockSpec((1, tpad, att), per_b), pl.BlockSpec((1, tpad, nh), per_b)])
    grid_spec = pltpu.PrefetchScalarGridSpec(
        num_scalar_prefetch=1,
        grid=(bs, ngroups),
        in_specs=in_specs,
        out_specs=pl.BlockSpec((1, ts, att), per_b),
        scratch_shapes=[pltpu.VMEM((LANES, att), BF16), pltpu.VMEM((SUBLANES, LANES), F32),
                        pltpu.VMEM((SUBLANES, LANES), F32), pltpu.VMEM((LANES, att), F32),
                        pltpu.VMEM((SUBLANES, LANES), F32)])
    return pl.pallas_call(
        functools.partial(_attn_sample_kernel, npg=npg, ts=ts, hd=hd, nh=nh),
        out_shape=jax.ShapeDtypeStruct((bs, ts, att), F32),
        grid_spec=grid_spec,
        compiler_params=_cparams(("arbitrary", "arbitrary")),
        name="attn_sample",
    )(page_table, *([ck] * npg), *([cv] * npg), *([clf] * npg), q, ks, vs, lfs)


def _merge_ffn_kernel(x_ref, rgn_ref, attn_ref, g1_ref, sh_ref, sc_ref, g2_ref, na_ref, nf_ref,
                      wo_ref, wg_ref, wu_ref, wd_ref, o_ref):
    attn_n = (_rms_rows(attn_ref[...]) * na_ref[...]).astype(BF16)
    mix_in = jnp.concatenate([rgn_ref[...], attn_n], axis=1)
    mix = jnp.dot(mix_in, wo_ref[...], preferred_element_type=F32)
    x1 = x_ref[...] + (1.0 + g1_ref[0]) * mix
    xn = _rms_rows(x1) * nf_ref[...]
    xn = (xn * (1.0 + sc_ref[0]) + sh_ref[0]).astype(BF16)
    gate = jnp.dot(xn, wg_ref[...], preferred_element_type=F32)
    up = jnp.dot(xn, wu_ref[...], preferred_element_type=F32)
    hid = (gate * jax.nn.sigmoid(gate) * up).astype(BF16)
    f = jnp.dot(hid, wd_ref[...], preferred_element_type=F32)
    o_ref[...] = x1 + (1.0 + g2_ref[0]) * f


def _merge_ffn(x2d, rgn, attn, mods4, na, nf, wo, wg, wu, wd, *, tm, rows_per_group, mod_rows):
    n, d = x2d.shape
    rg = rgn.shape[1]
    att = attn.shape[1]
    steps_per_group = rows_per_group // tm
    row = lambda i: (i, 0)
    const = lambda i: (0, 0)
    if mod_rows == 1:
        mod_spec = pl.BlockSpec((1, 1, d), lambda i: (i // steps_per_group, 0, 0))
    else:
        mod_spec = pl.BlockSpec((1, tm, d), lambda i: (0, i, 0))
    wspec = lambda w: pl.BlockSpec(w.shape, const, pipeline_mode=pl.Buffered(1))
    return pl.pallas_call(
        _merge_ffn_kernel,
        out_shape=jax.ShapeDtypeStruct((n, d), F32),
        grid=(n // tm,),
        in_specs=[pl.BlockSpec((tm, d), row), pl.BlockSpec((tm, rg), row), pl.BlockSpec((tm, att), row),
                  mod_spec, mod_spec, mod_spec, mod_spec,
                  pl.BlockSpec((1, att), const), pl.BlockSpec((1, d), const),
                  wspec(wo), wspec(wg), wspec(wu), wspec(wd)],
        out_specs=pl.BlockSpec((tm, d), row),
        compiler_params=_cparams(("arbitrary",)),
        name="merge_ffn",
    )(x2d, rgn, attn, *mods4, na, nf, wo, wg, wu, wd)


def _pair_blockdiag(wa, wx):
    nb, c, _ = wa.shape
    z = jnp.zeros((nb // 2, c, c), wa.dtype)

    def pair(w):
        w = w.reshape(nb // 2, 2, c, c)
        top = jnp.concatenate([w[:, 0], z], axis=2)
        bot = jnp.concatenate([z, w[:, 1]], axis=2)
        return jnp.concatenate([top, bot], axis=1)

    return jnp.concatenate([pair(wa), pair(wx)], axis=2).astype(BF16)


def kernel(x_prompt, x_sample, cache_k, cache_v, cache_logf, state_conv, state_h, page_table,
           c_prompt, c_sample, norm_mix, w_ada, b_ada, w_in, b_f, g_q, g_k, w_conv, b_conv,
           w_a, b_a, w_x, b_x, lam, norm_rg, norm_attn, w_out, norm_ffn,
           w_ffn_gate, w_ffn_up, w_ffn_down):
    depth = w_in.shape[0]
    assert depth == 1, "single-layer step"
    b, t, d = x_prompt.shape
    bs, ts, _ = x_sample.shape
    nh, hd = cache_k.shape[3], cache_k.shape[4]
    att = nh * hd
    rg = state_h.shape[-1]
    n_pool, page = cache_k.shape[1], cache_k.shape[2]
    in_cols = w_in.shape[2]
    assert in_cols == 2 * rg + 3 * att + nh

    tm_in = 512
    tm_rg = 512
    tq = 512
    tm_ffn = 256
    npg = 8

    wcols = -(-in_cols // LANES) * LANES
    w_in_b = jnp.pad(w_in[0], ((0, 0), (0, wcols - in_cols))).astype(BF16)
    ones_heads = jnp.kron(jnp.eye(nh, dtype=F32), jnp.ones((hd, hd), F32)).astype(BF16)
    gq_row = jnp.tile(g_q[0], nh)[None, :] * (hd ** -0.5)
    gk_row = jnp.tile(g_k[0], nh)[None, :]
    wa_pairs = _pair_blockdiag(w_a[0], w_x[0])
    wo_b = w_out[0].astype(BF16)
    wg_b = w_ffn_gate[0].astype(BF16)
    wu_b = w_ffn_up[0].astype(BF16)
    wd_b = w_ffn_down[0].astype(BF16)
    row = lambda a: a[0][None, :]

    mods = _ada(jnp.concatenate([c_prompt, c_sample], axis=0), w_ada[0], b_ada[0][None, :])
    mods = mods.reshape(b + bs, 6, d)
    mods_p = [mods[:b, i:i + 1] for i in range(6)]
    mods_s = [jnp.repeat(mods[b:, i], ts, axis=0)[None] for i in range(6)]

    xp2d = x_prompt.reshape(b * t, d)
    (xr, gr, kn, kb, v, qT, vT, logf, fcum) = _inproj_prompt(
        xp2d, mods_p[0], mods_p[1], row(norm_mix), w_in_b, ones_heads, gq_row, gk_row, row(b_f),
        b=b, t=t, tm=tm_in, rg=rg, att=att, hd=hd, nh=nh)
    rgn, h_last = _rg_prompt(xr, gr, w_conv[0], row(b_conv), wa_pairs, row(b_a), row(b_x), row(lam),
                             row(norm_rg), b=b, t=t, tm=tm_rg)
    attn = _attn_prompt(qT, kb.reshape(b, t, att), vT, fcum.reshape(b, t, nh), tq=tq, hd=hd, nh=nh)
    y_prompt = _merge_ffn(xp2d, rgn, attn.reshape(b * t, att), [mods_p[2], mods_p[3], mods_p[4], mods_p[5]],
                          row(norm_attn), row(norm_ffn), wo_b, wg_b, wu_b, wd_b,
                          tm=tm_ffn, rows_per_group=t, mod_rows=1)

    ns = bs * ts
    xs2d = x_sample.reshape(ns, d)
    (xr_s, gr_s, q_s, kn_s, v_s, logf_s) = _inproj_sample(
        xs2d, mods_s[0], mods_s[1], row(norm_mix), w_in_b, ones_heads, gq_row, gk_row, row(b_f),
        rg=rg, att=att, hd=hd, nh=nh)
    x_ext = jnp.concatenate([state_conv[0], xr_s.reshape(bs, ts, rg)], axis=1)
    xwin = [x_ext[:, j:j + ts].reshape(ns, rg) for j in range(CONV_WIDTH)]
    h0_rows = jnp.repeat(state_h[0], ts, axis=0)
    rgn_s, h_rows = _rg_sample(xwin, gr_s, h0_rows, w_conv[0], row(b_conv), wa_pairs, row(b_a), row(b_x),
                               row(lam), row(norm_rg), seg=ts)
    tpad = SUBLANES
    pad_t = lambda a: jnp.pad(a, ((0, 0), (0, tpad - ts), (0, 0)))
    attn_s = _attn_sample(
        page_table, cache_k[0].reshape(n_pool, page, att), cache_v[0].reshape(n_pool, page, att),
        cache_logf[0], q_s.reshape(bs, ts, att), pad_t(kn_s.reshape(bs, ts, att)),
        pad_t(v_s.reshape(bs, ts, att)), pad_t(logf_s.reshape(bs, ts, nh)), npg=npg, hd=hd, nh=nh)
    y_sample = _merge_ffn(xs2d, rgn_s, attn_s.reshape(ns, att), [mods_s[2], mods_s[3], mods_s[4], mods_s[5]],
                          row(norm_attn), row(norm_ffn), wo_b, wg_b, wu_b, wd_b,
                          tm=min(tm_ffn, ns), rows_per_group=ns, mod_rows=min(tm_ffn, ns))

    nc = CONV_WIDTH - 1
    return (y_prompt.reshape(b, t, d), y_sample.reshape(bs, ts, d),
            kn.reshape(1, b, t, nh, hd), v.reshape(1, b, t, nh, hd), logf.reshape(1, b, t, nh),
            xr.reshape(b, t, rg)[:, t - nc:, :][None], h_last.reshape(1, b, rg),
            kn_s.reshape(1, bs, ts, nh, hd), v_s.reshape(1, bs, ts, nh, hd), logf_s.reshape(1, bs, ts, nh),
            x_ext[:, ts:, :][None], h_rows.reshape(bs, ts, rg)[:, ts - 1, :][None])
```

```python
import functools

import jax
import jax.numpy as jnp
from jax import lax
from jax.experimental import pallas as pl
from jax.experimental.pallas import tpu as pltpu

EPS = 1e-6
RG_C = 8.0
CONV_WIDTH = 4
LANES = 128
SUBLANES = 8
NEG_BIG = -1e30
LOG2E = 1.4426950408889634
VMEM_LIMIT = 56 * 1024 * 1024

F32 = jnp.float32
BF16 = jnp.bfloat16


def _cparams(sem, vmem=VMEM_LIMIT):
    return pltpu.CompilerParams(dimension_semantics=sem, vmem_limit_bytes=vmem)


def _rms_rows(x):
    return x * lax.rsqrt(jnp.mean(x * x, axis=-1, keepdims=True) + EPS)


def _log_sigmoid(x):
    return -(jnp.maximum(-x, 0.0) + jnp.log1p(jnp.exp(-jnp.abs(x))))


def _cumsum_rows(x, seg=None):
    n = x.shape[0]
    seg = n if seg is None else seg
    row = lax.broadcasted_iota(jnp.int32, x.shape, 0)
    pos = row if seg == n else jnp.bitwise_and(row, seg - 1)
    d = 1
    while d < seg:
        x = x + jnp.where(pos >= d, pltpu.roll(x, d, axis=0), 0.0)
        d *= 2
    return x


def _scan_rows(a, u, seg=None):
    n = a.shape[0]
    seg = n if seg is None else seg
    row = lax.broadcasted_iota(jnp.int32, a.shape, 0)
    pos = row if seg == n else jnp.bitwise_and(row, seg - 1)
    d = 1
    while d < seg:
        m = pos >= d
        a_s = jnp.where(m, pltpu.roll(a, d, axis=0), 1.0)
        u_s = jnp.where(m, pltpu.roll(u, d, axis=0), 0.0)
        u = u + a * u_s
        a = a * a_s
        d *= 2
    return a, u


def _ada_kernel(c_ref, w_ref, b_ref, o_ref):
    c = c_ref[...]
    s = (c * jax.nn.sigmoid(c)).astype(BF16)
    o_ref[...] = jnp.dot(s, w_ref[...].astype(BF16), preferred_element_type=F32) + b_ref[...]


def _ada(c, w, b):
    m, d = c.shape
    n = w.shape[1]
    tn = n // 4
    return pl.pallas_call(
        _ada_kernel,
        out_shape=jax.ShapeDtypeStruct((m, n), F32),
        grid=(n // tn,),
        in_specs=[pl.BlockSpec((m, d), lambda j: (0, 0)),
                  pl.BlockSpec((d, tn), lambda j: (0, j)),
                  pl.BlockSpec((1, tn), lambda j: (0, j))],
        out_specs=pl.BlockSpec((m, tn), lambda j: (0, j)),
        compiler_params=_cparams(("arbitrary",)),
        name="ada_mod",
    )(c, w, b)


def _head_rms(t, ones_bf16, gain_row, hd):
    ssq = jnp.dot((t * t).astype(BF16), ones_bf16, preferred_element_type=F32)
    return t * lax.rsqrt(ssq * (1.0 / hd) + EPS) * gain_row


def _inproj_body(x_ref, shift_ref, scale_ref, g_ref, w_ref, ones_ref, gq_ref, gk_ref, bf_ref, rg, att, hd, nh):
    x = x_ref[...]
    xn = _rms_rows(x) * g_ref[...]
    xn = xn * (1.0 + scale_ref[0]) + shift_ref[0]
    z = jnp.dot(xn.astype(BF16), w_ref[...], preferred_element_type=F32)
    xr = z[:, 0:rg]
    gr = z[:, rg:2 * rg]
    q = z[:, 2 * rg:2 * rg + att]
    k = z[:, 2 * rg + att:2 * rg + 2 * att]
    v = z[:, 2 * rg + 2 * att:2 * rg + 3 * att]
    fl = z[:, 2 * rg + 3 * att:2 * rg + 3 * att + nh]
    ones = ones_ref[...]
    qn = _head_rms(q, ones, gq_ref[...], hd)
    kn = _head_rms(k, ones, gk_ref[...], hd)
    logf = _log_sigmoid(fl + bf_ref[...])
    return xr, gr, qn, kn, v, logf


def _inproj_prompt_kernel(x_ref, shift_ref, scale_ref, g_ref, w_ref, ones_ref, gq_ref, gk_ref, bf_ref, sel_ref,
                          xr_ref, gr_ref, knT_ref, ka_ref, vT32_ref, qT_ref, vT_ref, logf_ref,
                          carry_ref, *, rg, att, hd, nh):
    t = pl.program_id(1)

    @pl.when(t == 0)
    def _():
        carry_ref[...] = jnp.zeros_like(carry_ref)

    xr, gr, qn, kn, v, logf = _inproj_body(x_ref, shift_ref, scale_ref, g_ref, w_ref, ones_ref,
                                           gq_ref, gk_ref, bf_ref, rg, att, hd, nh)
    xr_ref[...] = xr
    gr_ref[...] = gr
    knT_ref[0] = kn.T
    v_t = v.T
    vT32_ref[0] = v_t
    qT_ref[0] = qn.T.astype(BF16)
    vT_ref[0] = v_t.astype(BF16)
    logf_ref[...] = logf
    fc = _cumsum_rows(logf) + carry_ref[0:1, 0:nh]
    n = fc.shape[0]
    carry_ref[0:1, 0:nh] = fc[n - 1:n, :]
    fb = fc * (-LOG2E)
    hi = fb.astype(BF16).astype(F32)
    r1 = fb - hi
    mid = r1.astype(BF16).astype(F32)
    lo = r1 - mid
    terms = jnp.concatenate([hi, mid, lo, jnp.zeros((n, LANES - 3 * nh), F32)], axis=1).astype(BF16)
    aug = jnp.dot(terms, sel_ref[...], preferred_element_type=F32).astype(BF16)
    kb = kn.astype(BF16)
    parts = []
    for hp in range(att // LANES):
        parts += [kb[:, hp * LANES:(hp + 1) * LANES], aug[:, hp * LANES:(hp + 1) * LANES]]
    ka_ref[...] = jnp.concatenate(parts, axis=1)


def _inproj_sample_kernel(x_ref, shift_ref, scale_ref, g_ref, w_ref, ones_ref, gq_ref, gk_ref, bf_ref,
                          xr_ref, gr_ref, q_ref, kn_ref, v_ref, logf_ref, *, rg, att, hd, nh):
    xr, gr, qn, kn, v, logf = _inproj_body(x_ref, shift_ref, scale_ref, g_ref, w_ref, ones_ref,
                                           gq_ref, gk_ref, bf_ref, rg, att, hd, nh)
    xr_ref[...] = xr
    gr_ref[...] = gr
    q_ref[...] = qn
    kn_ref[...] = kn
    v_ref[...] = v
    logf_ref[...] = logf


def _inproj_common_specs(d, wcols, att, nh, mod_rows, mod_map):
    const = lambda *_: (0, 0)
    return [pl.BlockSpec((1, mod_rows, d), mod_map),
            pl.BlockSpec((1, mod_rows, d), mod_map),
            pl.BlockSpec((1, d), const),
            pl.BlockSpec((d, wcols), const),
            pl.BlockSpec((att, att), const),
            pl.BlockSpec((1, att), const),
            pl.BlockSpec((1, att), const),
            pl.BlockSpec((1, nh), const)]


def _inproj_prompt(x2d, shift, scale, g, w, ones, gq, gk, bf, sel, *, b, t, tm, rg, att, hd, nh):
    n, d = x2d.shape
    nt = t // tm
    row = lambda bi, ti: (bi * nt + ti, 0)
    tr = lambda bi, ti: (bi, 0, ti)
    kern = functools.partial(_inproj_prompt_kernel, rg=rg, att=att, hd=hd, nh=nh)
    return pl.pallas_call(
        kern,
        out_shape=(jax.ShapeDtypeStruct((n, rg), F32), jax.ShapeDtypeStruct((n, rg), F32),
                   jax.ShapeDtypeStruct((b, att, t), F32), jax.ShapeDtypeStruct((n, 2 * att), BF16),
                   jax.ShapeDtypeStruct((b, att, t), F32),
                   jax.ShapeDtypeStruct((b, att, t), BF16), jax.ShapeDtypeStruct((b, att, t), BF16),
                   jax.ShapeDtypeStruct((n, nh), F32)),
        grid=(b, nt),
        in_specs=[pl.BlockSpec((tm, d), row)] + _inproj_common_specs(
            d, w.shape[1], att, nh, 1, lambda bi, ti: (bi, 0, 0)) + [pl.BlockSpec(sel.shape, lambda bi, ti: (0, 0))],
        out_specs=(pl.BlockSpec((tm, rg), row), pl.BlockSpec((tm, rg), row),
                   pl.BlockSpec((1, att, tm), tr), pl.BlockSpec((tm, 2 * att), row),
                   pl.BlockSpec((1, att, tm), tr),
                   pl.BlockSpec((1, att, tm), tr), pl.BlockSpec((1, att, tm), tr),
                   pl.BlockSpec((tm, nh), row)),
        scratch_shapes=[pltpu.VMEM((SUBLANES, LANES), F32)],
        compiler_params=_cparams(("arbitrary", "arbitrary")),
        name="inproj_prompt",
    )(x2d, shift, scale, g, w, ones, gq, gk, bf, sel)


def _inproj_sample(x2d, shift, scale, g, w, ones, gq, gk, bf, *, rg, att, hd, nh):
    n, d = x2d.shape
    row = lambda i: (0, 0)
    kern = functools.partial(_inproj_sample_kernel, rg=rg, att=att, hd=hd, nh=nh)
    return pl.pallas_call(
        kern,
        out_shape=(jax.ShapeDtypeStruct((n, rg), F32), jax.ShapeDtypeStruct((n, rg), F32),
                   jax.ShapeDtypeStruct((n, att), F32), jax.ShapeDtypeStruct((n, att), F32),
                   jax.ShapeDtypeStruct((n, att), F32), jax.ShapeDtypeStruct((n, nh), F32)),
        grid=(1,),
        in_specs=[pl.BlockSpec((n, d), row)] + _inproj_common_specs(
            d, w.shape[1], att, nh, n, lambda i: (0, 0, 0)),
        out_specs=(pl.BlockSpec((n, rg), row), pl.BlockSpec((n, rg), row),
                   pl.BlockSpec((n, att), row), pl.BlockSpec((n, att), row),
                   pl.BlockSpec((n, att), row), pl.BlockSpec((n, nh), row)),
        compiler_params=_cparams(("arbitrary",)),
        name="inproj_sample",
    )(x2d, shift, scale, g, w, ones, gq, gk, bf)


def _rg_gates(xc, wa_ref, ba, bx, lam):
    sp = jnp.maximum(-lam, 0.0) + jnp.log1p(jnp.exp(-jnp.abs(lam)))
    a_parts, u_parts = [], []
    npair = wa_ref.shape[0]
    pw = wa_ref.shape[1]
    for j in range(npair):
        sl = slice(pw * j, pw * (j + 1))
        xcj = xc[:, sl]
        g = jnp.dot(xcj.astype(BF16), wa_ref[j], preferred_element_type=F32)
        r = jax.nn.sigmoid(g[:, :pw] + ba[:, sl])
        i = jax.nn.sigmoid(g[:, pw:] + bx[:, sl])
        log_a = (-RG_C) * r * sp[:, sl]
        a = jnp.exp(log_a)
        a_parts.append(a)
        w2 = -jnp.tanh(log_a) * (a * a + 1.0)
        w = jnp.where(w2 > 0.0, w2 * lax.rsqrt(w2), 0.0)
        u_parts.append(w * (i * xcj))
    return a_parts, u_parts


def _rg_finish(h, gr, norm_row):
    out = h * jax.nn.gelu(gr)
    return (_rms_rows(out) * norm_row).astype(BF16)


def _rg_prompt_kernel(xr_ref, gr_ref, wc_ref, bc_ref, wa_ref, ba_ref, bx_ref, lam_ref, nrm_ref,
                      o_ref, hT_ref, xbuf_ref, h_ref, sa_ref, su_ref, *, tm):
    t = pl.program_id(1)

    @pl.when(t == 0)
    def _():
        xbuf_ref[0:SUBLANES, :] = jnp.zeros((SUBLANES, xbuf_ref.shape[1]), F32)
        h_ref[...] = jnp.zeros_like(h_ref)

    xbuf_ref[SUBLANES:SUBLANES + tm, :] = xr_ref[...]
    xfull = xbuf_ref[...]
    xc = bc_ref[...] + xfull[SUBLANES:, :] * wc_ref[CONV_WIDTH - 1:CONV_WIDTH, :]
    for dly in range(1, CONV_WIDTH):
        tap = CONV_WIDTH - 1 - dly
        xc = xc + pltpu.roll(xfull, dly, axis=0)[SUBLANES:, :] * wc_ref[tap:tap + 1, :]
    xbuf_ref[0:SUBLANES, :] = xbuf_ref[tm:tm + SUBLANES, :]

    a_parts, u_parts = _rg_gates(xc, wa_ref, ba_ref[...], bx_ref[...], lam_ref[...])
    pw = wa_ref.shape[1]
    for j, (a, u) in enumerate(zip(a_parts, u_parts)):
        big_a, big_u = _scan_rows(a, u, seg=SUBLANES)
        sa_ref[:, pw * j:pw * (j + 1)] = big_a
        su_ref[:, pw * j:pw * (j + 1)] = big_u

    def group(g, h_prev):
        r0 = pl.multiple_of(g * SUBLANES, SUBLANES)
        hg = sa_ref[pl.ds(r0, SUBLANES), :] * h_prev + su_ref[pl.ds(r0, SUBLANES), :]
        su_ref[pl.ds(r0, SUBLANES), :] = hg
        return hg[SUBLANES - 1:SUBLANES, :]

    h_last = lax.fori_loop(0, tm // SUBLANES, group, h_ref[0:1, :], unroll=8)
    h = su_ref[...]
    h_ref[0:1, :] = h_last
    hT_ref[0] = h_last
    o_ref[...] = _rg_finish(h, gr_ref[...], nrm_ref[...])


def _rg_prompt(xr, gr, wc, bc, wa, ba, bx, lam, nrm, *, b, t, tm):
    n, rg = xr.shape
    nt = t // tm
    row = lambda bi, ti: (bi * nt + ti, 0)
    const2 = lambda bi, ti: (0, 0)
    return pl.pallas_call(
        functools.partial(_rg_prompt_kernel, tm=tm),
        out_shape=(jax.ShapeDtypeStruct((n, rg), BF16), jax.ShapeDtypeStruct((b, 1, rg), F32)),
        grid=(b, nt),
        in_specs=[pl.BlockSpec((tm, rg), row), pl.BlockSpec((tm, rg), row),
                  pl.BlockSpec(wc.shape, const2), pl.BlockSpec((1, rg), const2),
                  pl.BlockSpec(wa.shape, lambda bi, ti: (0, 0, 0)),
                  pl.BlockSpec((1, rg), const2), pl.BlockSpec((1, rg), const2),
                  pl.BlockSpec((1, rg), const2), pl.BlockSpec((1, rg), const2)],
        out_specs=(pl.BlockSpec((tm, rg), row), pl.BlockSpec((1, 1, rg), lambda bi, ti: (bi, 0, 0))),
        scratch_shapes=[pltpu.VMEM((tm + SUBLANES, rg), F32), pltpu.VMEM((SUBLANES, rg), F32),
                        pltpu.VMEM((tm, rg), F32), pltpu.VMEM((tm, rg), F32)],
        compiler_params=_cparams(("arbitrary", "arbitrary")),
        name="rg_prompt",
    )(xr, gr, wc, bc, wa, ba, bx, lam, nrm)


def _rg_sample_kernel(x0_ref, x1_ref, x2_ref, x3_ref, gr_ref, h0_ref, wc_ref, bc_ref, wa_ref, ba_ref, bx_ref,
                      lam_ref, nrm_ref, o_ref, h_ref, *, seg):
    xs = (x0_ref, x1_ref, x2_ref, x3_ref)
    xc = bc_ref[...]
    for j in range(CONV_WIDTH):
        xc = xc + xs[j][...] * wc_ref[j:j + 1, :]
    a_parts, u_parts = _rg_gates(xc, wa_ref, ba_ref[...], bx_ref[...], lam_ref[...])
    pw = wa_ref.shape[1]
    h_parts = []
    for j, (a, u) in enumerate(zip(a_parts, u_parts)):
        big_a, big_u = _scan_rows(a, u, seg=seg)
        h_parts.append(big_a * h0_ref[:, pw * j:pw * (j + 1)] + big_u)
    h = jnp.concatenate(h_parts, axis=1)
    h_ref[...] = h
    o_ref[...] = _rg_finish(h, gr_ref[...], nrm_ref[...])


def _rg_sample(xwin, gr, h0_rows, wc, bc, wa, ba, bx, lam, nrm, *, seg):
    n, rg = gr.shape
    full = pl.BlockSpec((n, rg), lambda i: (0, 0))
    vec = pl.BlockSpec((1, rg), lambda i: (0, 0))
    return pl.pallas_call(
        functools.partial(_rg_sample_kernel, seg=seg),
        out_shape=(jax.ShapeDtypeStruct((n, rg), BF16), jax.ShapeDtypeStruct((n, rg), F32)),
        grid=(1,),
        in_specs=[full, full, full, full, full, full,
                  pl.BlockSpec(wc.shape, lambda i: (0, 0)), vec,
                  pl.BlockSpec(wa.shape, lambda i: (0, 0, 0)), vec, vec, vec, vec],
        out_specs=(full, full),
        compiler_params=_cparams(("arbitrary",)),
        name="rg_sample",
    )(*xwin, gr, h0_rows, wc, bc, wa, ba, bx, lam, nrm)


def _attn_prompt_kernel(qT_ref, ka_ref, vT_ref, o_ref, *, tq, tk, hd, npair):
    i = pl.program_id(2)
    pair = 2 * hd
    nhead = 2 * npair
    nsum = 16

    qs = []
    for pp in range(npair):
        qp = qT_ref[0, pp * pair:(pp + 1) * pair, :]
        rowq = lax.broadcasted_iota(jnp.int32, qp.shape, 0)
        zero = jnp.zeros_like(qp)
        for hh in range(2):
            qh = jnp.where(rowq < hd, qp, zero) if hh == 0 else jnp.where(rowq >= hd, qp, zero)
            pick = (jnp.where(rowq < 3 * hh + 3, 1.0, 0.0) - jnp.where(rowq < 3 * hh, 1.0, 0.0)).astype(BF16)
            qs.append(jnp.concatenate([qh, pick], axis=0))
    ones_rows = jnp.ones((nsum, tk), BF16)

    def scores(j, h):
        pp = h // 2
        start = pl.multiple_of(j * tk, tk)
        kblk = ka_ref[0, pl.ds(start, tk), 2 * pair * pp:2 * pair * (pp + 1)]
        return jnp.dot(kblk, qs[h], preferred_element_type=F32)

    def absorb(j, h, s, state, masked):
        m, acc = state
        start = pl.multiple_of(j * tk, tk)
        if masked:
            rk = lax.broadcasted_iota(jnp.int32, s.shape, 0)
            cq = lax.broadcasted_iota(jnp.int32, s.shape, 1)
            s = jnp.where(rk <= cq, s, NEG_BIG)
        m_new = jnp.maximum(m, jnp.max(s, axis=0, keepdims=True))
        p = jnp.exp2(s - m_new).astype(BF16)
        alpha = jnp.exp2(m - m_new)
        vblk = jnp.concatenate([vT_ref[0, h * hd:(h + 1) * hd, pl.ds(start, tk)], ones_rows], axis=0)
        pv = jnp.dot(vblk, p, preferred_element_type=F32)
        return m_new, alpha * acc + pv

    def body(j, c):
        s_cur, states = c
        new = []
        for h in range(nhead):
            s_next = scores(j, h + 1) if h + 1 < nhead else scores(j + 1, 0)
            new.append(absorb(j, h, s_cur, states[h], False))
            s_cur = s_next
        return s_cur, tuple(new)

    assert tq == tk
    init = tuple((jnp.full((1, tq), NEG_BIG, F32), jnp.zeros((hd + nsum, tq), F32)) for _ in range(nhead))
    s_cur, states = lax.fori_loop(0, i, body, (scores(0, 0), init))
    outs = []
    for h in range(nhead):
        s_next = scores(i, h + 1) if h + 1 < nhead else None
        _, acc = absorb(i, h, s_cur, states[h], True)
        outs.append(acc[0:hd] / acc[hd:hd + 1])
        s_cur = s_next
    o_ref[0] = jnp.concatenate(outs, axis=0).T


def _attn_prompt(qT, ka, vT, *, tq, tk, hd, npair):
    b, att, t = qT.shape
    width = 2 * hd * npair
    return pl.pallas_call(
        functools.partial(_attn_prompt_kernel, tq=tq, tk=tk, hd=hd, npair=npair),
        out_shape=jax.ShapeDtypeStruct((b, t, att), F32),
        grid=(b, att // width, t // tq),
        in_specs=[pl.BlockSpec((1, width, tq), lambda bi, hg, i: (bi, hg, i)),
                  pl.BlockSpec((1, t, 2 * width), lambda bi, hg, i: (bi, 0, hg)),
                  pl.BlockSpec((1, width, t), lambda bi, hg, i: (bi, hg, 0))],
        out_specs=pl.BlockSpec((1, tq, width), lambda bi, hg, i: (bi, i, hg)),
        compiler_params=_cparams(("arbitrary", "arbitrary", "arbitrary")),
        name="attn_prompt",
    )(qT, ka, vT)


def _cumsum_lanes(x):
    n = x.shape[1]
    lane = lax.broadcasted_iota(jnp.int32, x.shape, 1)
    d = 1
    while d < n:
        x = x + jnp.where(lane >= d, pltpu.roll(x, d, axis=1), 0.0)
        d *= 2
    return x


def _rows_as_lanes(a, height):
    r = a.shape[0]
    sq = jnp.concatenate([a, jnp.ones((LANES - r, LANES), F32)], axis=0).T
    return jnp.concatenate([sq] * (height // LANES), axis=0)


def _attn_sample_kernel(pt_ref, *refs, npg, ts, hd, nh):
    k_refs = refs[0:npg]
    v_refs = refs[npg:2 * npg]
    lf_refs = refs[2 * npg:3 * npg]
    q_ref, ks_ref, vs_ref, lfs_ref, o_ref, qp_ref, m_ref, l_ref, acc_ref, fcar_ref = refs[3 * npg:]
    g = pl.program_id(1)
    ng = pl.num_programs(1)
    att = nh * hd
    nrow = ts * nh
    page = k_refs[0].shape[2]

    @pl.when(g == 0)
    def _():
        q = q_ref[0]
        col_head = lax.broadcasted_iota(jnp.int32, (nh, att), 1) // hd
        row_head = lax.broadcasted_iota(jnp.int32, (nh, att), 0)
        blocks = [jnp.where(col_head == row_head, jnp.broadcast_to(q[tt:tt + 1, :], (nh, att)), 0.0)
                  for tt in range(ts)]
        qp_ref[...] = jnp.concatenate(blocks, axis=0).astype(BF16)
        m_ref[...] = jnp.full(m_ref.shape, NEG_BIG, F32)
        l_ref[...] = jnp.zeros_like(l_ref)
        acc_ref[...] = jnp.zeros_like(acc_ref)
        fcar_ref[...] = jnp.zeros_like(fcar_ref)

    def update(k_t, v_t, bias, valid):
        n = k_t.shape[1]
        s = jnp.dot(qp_ref[...], k_t, preferred_element_type=F32) + bias
        if valid is not None:
            s = jnp.where(valid, s, NEG_BIG)
        m_old = m_ref[...]
        m_new = jnp.maximum(m_old, jnp.max(s, axis=1, keepdims=True))
        p = jnp.exp(s - jnp.concatenate([m_new] * (n // LANES), axis=1))
        alpha = jnp.exp(m_old - m_new)
        l_ref[...] = alpha * l_ref[...] + jnp.sum(p, axis=1, keepdims=True)
        m_ref[...] = m_new
        p_pad = jnp.concatenate([p.astype(BF16), jnp.zeros((LANES - nrow, n), BF16)], axis=0)
        pv = lax.dot_general(v_t, p_pad, (((1,), (1,)), ((), ())), preferred_element_type=F32)
        acc_ref[...] = acc_ref[...] * _rows_as_lanes(alpha, att) + pv

    k_t = jnp.concatenate([r[0] for r in k_refs], axis=1).astype(BF16)
    v_t = jnp.concatenate([r[0] for r in v_refs], axis=1).astype(BF16)
    lf = jnp.concatenate([r[0] for r in lf_refs], axis=0)
    cum = _cumsum_lanes(lf)
    off = fcar_ref[...]
    pieces = []
    for pi in range(npg):
        cp = cum[pi * nh:(pi + 1) * nh, :]
        pieces.append(jnp.concatenate([-(cp + off)] * ts, axis=0))
        off = off + jnp.broadcast_to(cp[:, page - 1:page], (nh, LANES))
    fcar_ref[...] = off
    update(k_t, v_t, jnp.concatenate(pieces, axis=1), None)

    @pl.when(g == ng - 1)
    def _():
        zpad = jnp.zeros((LANES - ks_ref.shape[1], att), F32)
        ks_t = jnp.concatenate([ks_ref[0], zpad], axis=0).T.astype(BF16)
        vs_t = jnp.concatenate([vs_ref[0], zpad], axis=0).T.astype(BF16)
        fnew = _cumsum_lanes(lfs_ref[0]) + fcar_ref[...]
        bias = jnp.concatenate([-fnew] * ts, axis=0)
        key_t = lax.broadcasted_iota(jnp.int32, (nrow, LANES), 1)
        qry_t = lax.broadcasted_iota(jnp.int32, (nrow, LANES), 0) // nh
        valid = (key_t <= qry_t) & (key_t < ts)
        update(ks_t, vs_t, bias, valid)
        o_all = acc_ref[...] / _rows_as_lanes(l_ref[...], att)
        r_head = lax.broadcasted_iota(jnp.int32, (att, LANES), 0) // hd
        c_head = jnp.bitwise_and(lax.broadcasted_iota(jnp.int32, (att, LANES), 1), nh - 1)
        o_t = jnp.where(r_head == c_head, o_all, 0.0).T[0:nrow, :]
        o_ref[0] = jnp.sum(o_t.reshape(ts, nh, att), axis=1)


def _attn_sample(page_table, ck_t, cv_t, clf_t, q, ks, vs, lfs_t, *, npg, hd, nh):
    bs, ts, att = q.shape
    page = ck_t.shape[2]
    assert page == LANES
    n_pages = page_table.shape[1]
    ngroups = n_pages // npg
    tpad = ks.shape[1]
    nrow = ts * nh

    def page_map(ii):
        return lambda b, g, pt: (pt[b, g * npg + ii], 0, 0)

    per_b = lambda b, g, pt: (b, 0, 0)
    in_specs = ([pl.BlockSpec((1, att, page), page_map(ii)) for ii in range(npg)]
                + [pl.BlockSpec((1, att, page), page_map(ii)) for ii in range(npg)]
                + [pl.BlockSpec((1, nh, page), page_map(ii)) for ii in range(npg)]
                + [pl.BlockSpec((1, ts, att), per_b), pl.BlockSpec((1, tpad, att), per_b),
                   pl.BlockSpec((1, tpad, att), per_b), pl.BlockSpec((1, nh, LANES), per_b)])
    grid_spec = pltpu.PrefetchScalarGridSpec(
        num_scalar_prefetch=1,
        grid=(bs, ngroups),
        in_specs=in_specs,
        out_specs=pl.BlockSpec((1, ts, att), per_b),
        scratch_shapes=[pltpu.VMEM((nrow, att), BF16), pltpu.VMEM((nrow, LANES), F32),
                        pltpu.VMEM((nrow, LANES), F32), pltpu.VMEM((att, LANES), F32),
                        pltpu.VMEM((nh, LANES), F32)])
    return pl.pallas_call(
        functools.partial(_attn_sample_kernel, npg=npg, ts=ts, hd=hd, nh=nh),
        out_shape=jax.ShapeDtypeStruct((bs, ts, att), F32),
        grid_spec=grid_spec,
        compiler_params=_cparams(("arbitrary", "arbitrary")),
        name="attn_sample",
    )(page_table, *([ck_t] * npg), *([cv_t] * npg), *([clf_t] * npg), q, ks, vs, lfs_t)


def _merge_ffn_kernel(x_ref, rgn_ref, attn_ref, g1_ref, sh_ref, sc_ref, g2_ref, na_ref, nf_ref,
                      wo_ref, wg_ref, wu_ref, wd_ref, o_ref):
    attn_n = (_rms_rows(attn_ref[...]) * na_ref[...]).astype(BF16)
    mix_in = jnp.concatenate([rgn_ref[...], attn_n], axis=1)
    mix = jnp.dot(mix_in, wo_ref[...], preferred_element_type=F32)
    x1 = x_ref[...] + (1.0 + g1_ref[0]) * mix
    xn = _rms_rows(x1) * nf_ref[...]
    xn = (xn * (1.0 + sc_ref[0]) + sh_ref[0]).astype(BF16)
    gate = jnp.dot(xn, wg_ref[...], preferred_element_type=F32)
    up = jnp.dot(xn, wu_ref[...], preferred_element_type=F32)
    hid = (gate * jax.nn.sigmoid(gate) * up).astype(BF16)
    f = jnp.dot(hid, wd_ref[...], preferred_element_type=F32)
    o_ref[...] = x1 + (1.0 + g2_ref[0]) * f


def _merge_ffn(x2d, rgn, attn, mods4, na, nf, wo, wg, wu, wd, *, tm, rows_per_group, mod_rows):
    n, d = x2d.shape
    rg = rgn.shape[1]
    att = attn.shape[1]
    steps_per_group = rows_per_group // tm
    row = lambda i: (i, 0)
    const = lambda i: (0, 0)
    if mod_rows == 1:
        mod_spec = pl.BlockSpec((1, 1, d), lambda i: (i // steps_per_group, 0, 0))
    else:
        mod_spec = pl.BlockSpec((1, tm, d), lambda i: (0, i, 0))
    wspec = lambda w: pl.BlockSpec(w.shape, const, pipeline_mode=pl.Buffered(1))
    return pl.pallas_call(
        _merge_ffn_kernel,
        out_shape=jax.ShapeDtypeStruct((n, d), F32),
        grid=(n // tm,),
        in_specs=[pl.BlockSpec((tm, d), row), pl.BlockSpec((tm, rg), row), pl.BlockSpec((tm, att), row),
                  mod_spec, mod_spec, mod_spec, mod_spec,
                  pl.BlockSpec((1, att), const), pl.BlockSpec((1, d), const),
                  wspec(wo), wspec(wg), wspec(wu), wspec(wd)],
        out_specs=pl.BlockSpec((tm, d), row),
        compiler_params=_cparams(("arbitrary",)),
        name="merge_ffn",
    )(x2d, rgn, attn, *mods4, na, nf, wo, wg, wu, wd)


def _pair_blockdiag(wa, wx):
    nb, c, _ = wa.shape
    z = jnp.zeros((nb // 2, c, c), wa.dtype)

    def pair(w):
        w = w.reshape(nb // 2, 2, c, c)
        top = jnp.concatenate([w[:, 0], z], axis=2)
        bot = jnp.concatenate([z, w[:, 1]], axis=2)
        return jnp.concatenate([top, bot], axis=1)

    return jnp.concatenate([pair(wa), pair(wx)], axis=2).astype(BF16)


def kernel(x_prompt, x_sample, cache_k, cache_v, cache_logf, state_conv, state_h, page_table,
           c_prompt, c_sample, norm_mix, w_ada, b_ada, w_in, b_f, g_q, g_k, w_conv, b_conv,
           w_a, b_a, w_x, b_x, lam, norm_rg, norm_attn, w_out, norm_ffn,
           w_ffn_gate, w_ffn_up, w_ffn_down):
    depth = w_in.shape[0]
    assert depth == 1, "single-layer step"
    b, t, d = x_prompt.shape
    bs, ts, _ = x_sample.shape
    nh, hd = cache_k.shape[3], cache_k.shape[4]
    att = nh * hd
    rg = state_h.shape[-1]
    n_pool, page = cache_k.shape[1], cache_k.shape[2]
    in_cols = w_in.shape[2]
    assert in_cols == 2 * rg + 3 * att + nh

    tm_in = 512
    tm_rg = 512
    tq = 512
    tk = 512
    tm_ffn = 256
    npg = 8

    wcols = -(-in_cols // LANES) * LANES
    w_in_b = jnp.pad(w_in[0], ((0, 0), (0, wcols - in_cols))).astype(BF16)
    ones_heads = jnp.kron(jnp.eye(nh, dtype=F32), jnp.ones((hd, hd), F32)).astype(BF16)
    gq_row = jnp.tile(g_q[0], nh)[None, :] * (hd ** -0.5)
    gq_row2 = gq_row * LOG2E
    gk_row = jnp.tile(g_k[0], nh)[None, :]
    assert 2 * hd == LANES and 3 * nh <= LANES
    sel_np = [[0.0] * att for _ in range(LANES)]
    for h in range(nh):
        for kk in range(3):
            sel_np[kk * nh + h][LANES * (h // 2) + 3 * (h % 2) + kk] = 1.0
    sel_terms = jnp.array(sel_np, BF16)
    wa_pairs = _pair_blockdiag(w_a[0], w_x[0])
    wo_b = w_out[0].astype(BF16)
    wg_b = w_ffn_gate[0].astype(BF16)
    wu_b = w_ffn_up[0].astype(BF16)
    wd_b = w_ffn_down[0].astype(BF16)
    row = lambda a: a[0][None, :]

    mods = _ada(jnp.concatenate([c_prompt, c_sample], axis=0), w_ada[0], b_ada[0][None, :])
    mods = mods.reshape(b + bs, 6, d)
    mods_p = [mods[:b, i:i + 1] for i in range(6)]
    mods_s = [jnp.repeat(mods[b:, i], ts, axis=0)[None] for i in range(6)]

    xp2d = x_prompt.reshape(b * t, d)
    (xr, gr, knT, ka, vT32, qT, vT, logf) = _inproj_prompt(
        xp2d, mods_p[0], mods_p[1], row(norm_mix), w_in_b, ones_heads, gq_row2, gk_row, row(b_f), sel_terms,
        b=b, t=t, tm=tm_in, rg=rg, att=att, hd=hd, nh=nh)
    rgn, h_last = _rg_prompt(xr, gr, w_conv[0], row(b_conv), wa_pairs, row(b_a), row(b_x), row(lam),
                             row(norm_rg), b=b, t=t, tm=tm_rg)
    attn = _attn_prompt(qT, ka.reshape(b, t, 2 * att), vT, tq=tq, tk=tk, hd=hd, npair=2)
    y_prompt = _merge_ffn(xp2d, rgn, attn.reshape(b * t, att), [mods_p[2], mods_p[3], mods_p[4], mods_p[5]],
                          row(norm_attn), row(norm_ffn), wo_b, wg_b, wu_b, wd_b,
                          tm=tm_ffn, rows_per_group=t, mod_rows=1)

    ns = bs * ts
    xs2d = x_sample.reshape(ns, d)
    (xr_s, gr_s, q_s, kn_s, v_s, logf_s) = _inproj_sample(
        xs2d, mods_s[0], mods_s[1], row(norm_mix), w_in_b, ones_heads, gq_row, gk_row, row(b_f),
        rg=rg, att=att, hd=hd, nh=nh)
    x_ext = jnp.concatenate([state_conv[0], xr_s.reshape(bs, ts, rg)], axis=1)
    xwin = [x_ext[:, j:j + ts].reshape(ns, rg) for j in range(CONV_WIDTH)]
    h0_rows = jnp.repeat(state_h[0], ts, axis=0)
    rgn_s, h_rows = _rg_sample(xwin, gr_s, h0_rows, w_conv[0], row(b_conv), wa_pairs, row(b_a), row(b_x),
                               row(lam), row(norm_rg), seg=ts)
    tpad = SUBLANES
    pad_t = lambda a: jnp.pad(a, ((0, 0), (0, tpad - ts), (0, 0)))
    ck_t = jnp.transpose(cache_k[0], (0, 2, 3, 1)).reshape(n_pool, att, page)
    cv_t = jnp.transpose(cache_v[0], (0, 2, 3, 1)).reshape(n_pool, att, page)
    clf_t = jnp.transpose(cache_logf[0], (0, 2, 1))
    lfs_t = jnp.pad(jnp.transpose(logf_s.reshape(bs, ts, nh), (0, 2, 1)), ((0, 0), (0, 0), (0, LANES - ts)))
    attn_s = _attn_sample(
        page_table, ck_t, cv_t, clf_t, q_s.reshape(bs, ts, att), pad_t(kn_s.reshape(bs, ts, att)),
        pad_t(v_s.reshape(bs, ts, att)), lfs_t, npg=npg, hd=hd, nh=nh)
    y_sample = _merge_ffn(xs2d, rgn_s, attn_s.reshape(ns, att), [mods_s[2], mods_s[3], mods_s[4], mods_s[5]],
                          row(norm_attn), row(norm_ffn), wo_b, wg_b, wu_b, wd_b,
                          tm=min(tm_ffn, ns), rows_per_group=ns, mod_rows=min(tm_ffn, ns))

    nc = CONV_WIDTH - 1
    heads_last = lambda a: jnp.transpose(a.reshape(b, nh, hd, t), (0, 3, 1, 2))[None]
    return (y_prompt.reshape(b, t, d), y_sample.reshape(bs, ts, d),
            heads_last(knT), heads_last(vT32), logf.reshape(1, b, t, nh),
            xr.reshape(b, t, rg)[:, t - nc:, :][None], h_last.reshape(1, b, rg),
            kn_s.reshape(1, bs, ts, nh, hd), v_s.reshape(1, bs, ts, nh, hd), logf_s.reshape(1, bs, ts, nh),
            x_ext[:, ts:, :][None], h_rows.reshape(bs, ts, rg)[:, ts - 1, :][None])
```

```python
import functools

import jax
import jax.numpy as jnp
from jax import lax
from jax.experimental import pallas as pl
from jax.experimental.pallas import tpu as pltpu

EPS = 1e-6
RG_C = 8.0
CONV_WIDTH = 4
LANES = 128
SUBLANES = 8
NEG_BIG = -1e30
LOG2E = 1.4426950408889634
VMEM_LIMIT = 56 * 1024 * 1024

F32 = jnp.float32
BF16 = jnp.bfloat16


def _cparams(sem, vmem=VMEM_LIMIT):
    return pltpu.CompilerParams(dimension_semantics=sem, vmem_limit_bytes=vmem)


def _rms_rows(x):
    return x * lax.rsqrt(jnp.mean(x * x, axis=-1, keepdims=True) + EPS)


def _log_sigmoid(x):
    return -(jnp.maximum(-x, 0.0) + jnp.log1p(jnp.exp(-jnp.abs(x))))


def _cumsum_rows(x, seg=None):
    n = x.shape[0]
    seg = n if seg is None else seg
    row = lax.broadcasted_iota(jnp.int32, x.shape, 0)
    pos = row if seg == n else jnp.bitwise_and(row, seg - 1)
    d = 1
    while d < seg:
        x = x + jnp.where(pos >= d, pltpu.roll(x, d, axis=0), 0.0)
        d *= 2
    return x


def _scan_rows(a, u, seg=None):
    n = a.shape[0]
    seg = n if seg is None else seg
    row = lax.broadcasted_iota(jnp.int32, a.shape, 0)
    pos = row if seg == n else jnp.bitwise_and(row, seg - 1)
    d = 1
    while d < seg:
        m = pos >= d
        a_s = jnp.where(m, pltpu.roll(a, d, axis=0), 1.0)
        u_s = jnp.where(m, pltpu.roll(u, d, axis=0), 0.0)
        u = u + a * u_s
        a = a * a_s
        d *= 2
    return a, u


def _ada_kernel(c_ref, w_ref, b_ref, o_ref):
    c = c_ref[...]
    s = (c * jax.nn.sigmoid(c)).astype(BF16)
    o_ref[...] = jnp.dot(s, w_ref[...].astype(BF16), preferred_element_type=F32) + b_ref[...]


def _ada(c, w, b):
    m, d = c.shape
    n = w.shape[1]
    tn = n // 4
    return pl.pallas_call(
        _ada_kernel,
        out_shape=jax.ShapeDtypeStruct((m, n), F32),
        grid=(n // tn,),
        in_specs=[pl.BlockSpec((m, d), lambda j: (0, 0)),
                  pl.BlockSpec((d, tn), lambda j: (0, j)),
                  pl.BlockSpec((1, tn), lambda j: (0, j))],
        out_specs=pl.BlockSpec((m, tn), lambda j: (0, j)),
        compiler_params=_cparams(("arbitrary",)),
        name="ada_mod",
    )(c, w, b)


def _head_rms(t, ones_bf16, gain_row, hd):
    ssq = jnp.dot((t * t).astype(BF16), ones_bf16, preferred_element_type=F32)
    return t * lax.rsqrt(ssq * (1.0 / hd) + EPS) * gain_row


def _inproj_body(x_ref, shift_ref, scale_ref, g_ref, w_ref, ones_ref, gq_ref, gk_ref, bf_ref, rg, att, hd, nh):
    x = x_ref[...]
    xn = _rms_rows(x) * g_ref[...]
    xn = xn * (1.0 + scale_ref[0]) + shift_ref[0]
    z = jnp.dot(xn.astype(BF16), w_ref[...], preferred_element_type=F32)
    xr = z[:, 0:rg]
    gr = z[:, rg:2 * rg]
    q = z[:, 2 * rg:2 * rg + att]
    k = z[:, 2 * rg + att:2 * rg + 2 * att]
    v = z[:, 2 * rg + 2 * att:2 * rg + 3 * att]
    fl = z[:, 2 * rg + 3 * att:2 * rg + 3 * att + nh]
    ones = ones_ref[...]
    qn = _head_rms(q, ones, gq_ref[...], hd)
    kn = _head_rms(k, ones, gk_ref[...], hd)
    logf = _log_sigmoid(fl + bf_ref[...])
    return xr, gr, qn, kn, v, logf


def _inproj_prompt_kernel(x_ref, shift_ref, scale_ref, g_ref, w_ref, ones_ref, gq_ref, gk_ref, bf_ref, sel_ref,
                          xr_ref, gr_ref, knT_ref, ka_ref, vT32_ref, qT_ref, vT_ref, logf_ref,
                          carry_ref, *, rg, att, hd, nh):
    t = pl.program_id(1)

    @pl.when(t == 0)
    def _():
        carry_ref[...] = jnp.zeros_like(carry_ref)

    xr, gr, qn, kn, v, logf = _inproj_body(x_ref, shift_ref, scale_ref, g_ref, w_ref, ones_ref,
                                           gq_ref, gk_ref, bf_ref, rg, att, hd, nh)
    xr_ref[...] = xr
    gr_ref[...] = gr
    knT_ref[0] = kn.T
    v_t = v.T
    vT32_ref[0] = v_t
    qT_ref[0] = qn.T.astype(BF16)
    vT_ref[0] = v_t.astype(BF16)
    logf_ref[...] = logf
    fc = _cumsum_rows(logf) + carry_ref[0:1, 0:nh]
    n = fc.shape[0]
    carry_ref[0:1, 0:nh] = fc[n - 1:n, :]
    fb = fc * (-LOG2E)
    hi = fb.astype(BF16).astype(F32)
    r1 = fb - hi
    mid = r1.astype(BF16).astype(F32)
    lo = r1 - mid
    terms = jnp.concatenate([hi, mid, lo, jnp.zeros((n, LANES - 3 * nh), F32)], axis=1).astype(BF16)
    aug = jnp.dot(terms, sel_ref[...], preferred_element_type=F32).astype(BF16)
    kb = kn.astype(BF16)
    parts = []
    for hp in range(att // LANES):
        parts += [kb[:, hp * LANES:(hp + 1) * LANES], aug[:, hp * LANES:(hp + 1) * LANES]]
    ka_ref[...] = jnp.concatenate(parts, axis=1)


def _inproj_sample_kernel(x_ref, shift_ref, scale_ref, g_ref, w_ref, ones_ref, gq_ref, gk_ref, bf_ref,
                          xr_ref, gr_ref, q_ref, kn_ref, v_ref, logf_ref, *, rg, att, hd, nh):
    xr, gr, qn, kn, v, logf = _inproj_body(x_ref, shift_ref, scale_ref, g_ref, w_ref, ones_ref,
                                           gq_ref, gk_ref, bf_ref, rg, att, hd, nh)
    xr_ref[...] = xr
    gr_ref[...] = gr
    q_ref[...] = qn
    kn_ref[...] = kn
    v_ref[...] = v
    logf_ref[...] = logf


def _inproj_common_specs(d, wcols, att, nh, mod_rows, mod_map):
    const = lambda *_: (0, 0)
    return [pl.BlockSpec((1, mod_rows, d), mod_map),
            pl.BlockSpec((1, mod_rows, d), mod_map),
            pl.BlockSpec((1, d), const),
            pl.BlockSpec((d, wcols), const),
            pl.BlockSpec((att, att), const),
            pl.BlockSpec((1, att), const),
            pl.BlockSpec((1, att), const),
            pl.BlockSpec((1, nh), const)]


def _inproj_prompt(x2d, shift, scale, g, w, ones, gq, gk, bf, sel, *, b, t, tm, rg, att, hd, nh):
    n, d = x2d.shape
    nt = t // tm
    row = lambda bi, ti: (bi * nt + ti, 0)
    tr = lambda bi, ti: (bi, 0, ti)
    kern = functools.partial(_inproj_prompt_kernel, rg=rg, att=att, hd=hd, nh=nh)
    return pl.pallas_call(
        kern,
        out_shape=(jax.ShapeDtypeStruct((n, rg), F32), jax.ShapeDtypeStruct((n, rg), F32),
                   jax.ShapeDtypeStruct((b, att, t), F32), jax.ShapeDtypeStruct((n, 2 * att), BF16),
                   jax.ShapeDtypeStruct((b, att, t), F32),
                   jax.ShapeDtypeStruct((b, att, t), BF16), jax.ShapeDtypeStruct((b, att, t), BF16),
                   jax.ShapeDtypeStruct((n, nh), F32)),
        grid=(b, nt),
        in_specs=[pl.BlockSpec((tm, d), row)] + _inproj_common_specs(
            d, w.shape[1], att, nh, 1, lambda bi, ti: (bi, 0, 0)) + [pl.BlockSpec(sel.shape, lambda bi, ti: (0, 0))],
        out_specs=(pl.BlockSpec((tm, rg), row), pl.BlockSpec((tm, rg), row),
                   pl.BlockSpec((1, att, tm), tr), pl.BlockSpec((tm, 2 * att), row),
                   pl.BlockSpec((1, att, tm), tr),
                   pl.BlockSpec((1, att, tm), tr), pl.BlockSpec((1, att, tm), tr),
                   pl.BlockSpec((tm, nh), row)),
        scratch_shapes=[pltpu.VMEM((SUBLANES, LANES), F32)],
        compiler_params=_cparams(("arbitrary", "arbitrary")),
        name="inproj_prompt",
    )(x2d, shift, scale, g, w, ones, gq, gk, bf, sel)


def _inproj_sample(x2d, shift, scale, g, w, ones, gq, gk, bf, *, rg, att, hd, nh):
    n, d = x2d.shape
    row = lambda i: (0, 0)
    kern = functools.partial(_inproj_sample_kernel, rg=rg, att=att, hd=hd, nh=nh)
    return pl.pallas_call(
        kern,
        out_shape=(jax.ShapeDtypeStruct((n, rg), F32), jax.ShapeDtypeStruct((n, rg), F32),
                   jax.ShapeDtypeStruct((n, att), F32), jax.ShapeDtypeStruct((n, att), F32),
                   jax.ShapeDtypeStruct((n, att), F32), jax.ShapeDtypeStruct((n, nh), F32)),
        grid=(1,),
        in_specs=[pl.BlockSpec((n, d), row)] + _inproj_common_specs(
            d, w.shape[1], att, nh, n, lambda i: (0, 0, 0)),
        out_specs=(pl.BlockSpec((n, rg), row), pl.BlockSpec((n, rg), row),
                   pl.BlockSpec((n, att), row), pl.BlockSpec((n, att), row),
                   pl.BlockSpec((n, att), row), pl.BlockSpec((n, nh), row)),
        compiler_params=_cparams(("arbitrary",)),
        name="inproj_sample",
    )(x2d, shift, scale, g, w, ones, gq, gk, bf)


def _rg_gates(xc, wa_ref, ba, bx, lam):
    sp = jnp.maximum(-lam, 0.0) + jnp.log1p(jnp.exp(-jnp.abs(lam)))
    a_parts, u_parts = [], []
    npair = wa_ref.shape[0]
    pw = wa_ref.shape[1]
    for j in range(npair):
        sl = slice(pw * j, pw * (j + 1))
        xcj = xc[:, sl]
        g = jnp.dot(xcj.astype(BF16), wa_ref[j], preferred_element_type=F32)
        r = jax.nn.sigmoid(g[:, :pw] + ba[:, sl])
        i = jax.nn.sigmoid(g[:, pw:] + bx[:, sl])
        log_a = (-RG_C) * r * sp[:, sl]
        a = jnp.exp(log_a)
        a_parts.append(a)
        w2 = -jnp.tanh(log_a) * (a * a + 1.0)
        w = jnp.where(w2 > 0.0, w2 * lax.rsqrt(w2), 0.0)
        u_parts.append(w * (i * xcj))
    return a_parts, u_parts


def _rg_finish(h, gr, norm_row):
    out = h * jax.nn.gelu(gr)
    return (_rms_rows(out) * norm_row).astype(BF16)


def _rg_prompt_kernel(xr_ref, gr_ref, wc_ref, bc_ref, wa_ref, ba_ref, bx_ref, lam_ref, nrm_ref,
                      o_ref, hT_ref, xbuf_ref, h_ref, sa_ref, su_ref, *, tm):
    t = pl.program_id(1)

    @pl.when(t == 0)
    def _():
        xbuf_ref[0:SUBLANES, :] = jnp.zeros((SUBLANES, xbuf_ref.shape[1]), F32)
        h_ref[...] = jnp.zeros_like(h_ref)

    xbuf_ref[SUBLANES:SUBLANES + tm, :] = xr_ref[...]
    xfull = xbuf_ref[...]
    xc = bc_ref[...] + xfull[SUBLANES:, :] * wc_ref[CONV_WIDTH - 1:CONV_WIDTH, :]
    for dly in range(1, CONV_WIDTH):
        tap = CONV_WIDTH - 1 - dly
        xc = xc + pltpu.roll(xfull, dly, axis=0)[SUBLANES:, :] * wc_ref[tap:tap + 1, :]
    xbuf_ref[0:SUBLANES, :] = xbuf_ref[tm:tm + SUBLANES, :]

    a_parts, u_parts = _rg_gates(xc, wa_ref, ba_ref[...], bx_ref[...], lam_ref[...])
    pw = wa_ref.shape[1]
    for j, (a, u) in enumerate(zip(a_parts, u_parts)):
        big_a, big_u = _scan_rows(a, u, seg=SUBLANES)
        sa_ref[:, pw * j:pw * (j + 1)] = big_a
        su_ref[:, pw * j:pw * (j + 1)] = big_u

    def group(g, h_prev):
        r0 = pl.multiple_of(g * SUBLANES, SUBLANES)
        hg = sa_ref[pl.ds(r0, SUBLANES), :] * h_prev + su_ref[pl.ds(r0, SUBLANES), :]
        su_ref[pl.ds(r0, SUBLANES), :] = hg
        return hg[SUBLANES - 1:SUBLANES, :]

    h_last = lax.fori_loop(0, tm // SUBLANES, group, h_ref[0:1, :], unroll=8)
    h = su_ref[...]
    h_ref[0:1, :] = h_last
    hT_ref[0] = h_last
    o_ref[...] = _rg_finish(h, gr_ref[...], nrm_ref[...])


def _rg_prompt(xr, gr, wc, bc, wa, ba, bx, lam, nrm, *, b, t, tm):
    n, rg = xr.shape
    nt = t // tm
    row = lambda bi, ti: (bi * nt + ti, 0)
    const2 = lambda bi, ti: (0, 0)
    return pl.pallas_call(
        functools.partial(_rg_prompt_kernel, tm=tm),
        out_shape=(jax.ShapeDtypeStruct((n, rg), BF16), jax.ShapeDtypeStruct((b, 1, rg), F32)),
        grid=(b, nt),
        in_specs=[pl.BlockSpec((tm, rg), row), pl.BlockSpec((tm, rg), row),
                  pl.BlockSpec(wc.shape, const2), pl.BlockSpec((1, rg), const2),
                  pl.BlockSpec(wa.shape, lambda bi, ti: (0, 0, 0)),
                  pl.BlockSpec((1, rg), const2), pl.BlockSpec((1, rg), const2),
                  pl.BlockSpec((1, rg), const2), pl.BlockSpec((1, rg), const2)],
        out_specs=(pl.BlockSpec((tm, rg), row), pl.BlockSpec((1, 1, rg), lambda bi, ti: (bi, 0, 0))),
        scratch_shapes=[pltpu.VMEM((tm + SUBLANES, rg), F32), pltpu.VMEM((SUBLANES, rg), F32),
                        pltpu.VMEM((tm, rg), F32), pltpu.VMEM((tm, rg), F32)],
        compiler_params=_cparams(("arbitrary", "arbitrary")),
        name="rg_prompt",
    )(xr, gr, wc, bc, wa, ba, bx, lam, nrm)


def _rg_sample_kernel(x0_ref, x1_ref, x2_ref, x3_ref, gr_ref, h0_ref, wc_ref, bc_ref, wa_ref, ba_ref, bx_ref,
                      lam_ref, nrm_ref, o_ref, h_ref, *, seg):
    xs = (x0_ref, x1_ref, x2_ref, x3_ref)
    xc = bc_ref[...]
    for j in range(CONV_WIDTH):
        xc = xc + xs[j][...] * wc_ref[j:j + 1, :]
    a_parts, u_parts = _rg_gates(xc, wa_ref, ba_ref[...], bx_ref[...], lam_ref[...])
    pw = wa_ref.shape[1]
    h_parts = []
    for j, (a, u) in enumerate(zip(a_parts, u_parts)):
        big_a, big_u = _scan_rows(a, u, seg=seg)
        h_parts.append(big_a * h0_ref[:, pw * j:pw * (j + 1)] + big_u)
    h = jnp.concatenate(h_parts, axis=1)
    h_ref[...] = h
    o_ref[...] = _rg_finish(h, gr_ref[...], nrm_ref[...])


def _rg_sample(xwin, gr, h0_rows, wc, bc, wa, ba, bx, lam, nrm, *, seg):
    n, rg = gr.shape
    full = pl.BlockSpec((n, rg), lambda i: (0, 0))
    vec = pl.BlockSpec((1, rg), lambda i: (0, 0))
    return pl.pallas_call(
        functools.partial(_rg_sample_kernel, seg=seg),
        out_shape=(jax.ShapeDtypeStruct((n, rg), BF16), jax.ShapeDtypeStruct((n, rg), F32)),
        grid=(1,),
        in_specs=[full, full, full, full, full, full,
                  pl.BlockSpec(wc.shape, lambda i: (0, 0)), vec,
                  pl.BlockSpec(wa.shape, lambda i: (0, 0, 0)), vec, vec, vec, vec],
        out_specs=(full, full),
        compiler_params=_cparams(("arbitrary",)),
        name="rg_sample",
    )(*xwin, gr, h0_rows, wc, bc, wa, ba, bx, lam, nrm)


def _attn_prompt_kernel(qT_ref, ka_ref, vT_ref, o_ref, *, tq, tk, hd, npair):
    i = pl.program_id(2)
    pair = 2 * hd
    nhead = 2 * npair
    nsum = 16

    qs = []
    for pp in range(npair):
        qp = qT_ref[0, pp * pair:(pp + 1) * pair, :]
        rowq = lax.broadcasted_iota(jnp.int32, qp.shape, 0)
        zero = jnp.zeros_like(qp)
        for hh in range(2):
            qh = jnp.where(rowq < hd, qp, zero) if hh == 0 else jnp.where(rowq >= hd, qp, zero)
            pick = (jnp.where(rowq < 3 * hh + 3, 1.0, 0.0) - jnp.where(rowq < 3 * hh, 1.0, 0.0)).astype(BF16)
            qs.append(jnp.concatenate([qh, pick], axis=0))
    ones_rows = jnp.ones((nsum, tk), BF16)

    def scores(j, h):
        pp = h // 2
        start = pl.multiple_of(j * tk, tk)
        kblk = ka_ref[0, pl.ds(start, tk), 2 * pair * pp:2 * pair * (pp + 1)]
        return jnp.dot(kblk, qs[h], preferred_element_type=F32)

    def absorb(j, h, s, state, masked):
        m, acc = state
        start = pl.multiple_of(j * tk, tk)
        if masked:
            rk = lax.broadcasted_iota(jnp.int32, s.shape, 0)
            cq = lax.broadcasted_iota(jnp.int32, s.shape, 1)
            s = jnp.where(rk <= cq, s, NEG_BIG)
        m_new = jnp.maximum(m, jnp.max(s, axis=0, keepdims=True))
        p = jnp.exp2(s - m_new).astype(BF16)
        alpha = jnp.exp2(m - m_new)
        vblk = jnp.concatenate([vT_ref[0, h * hd:(h + 1) * hd, pl.ds(start, tk)], ones_rows], axis=0)
        pv = jnp.dot(vblk, p, preferred_element_type=F32)
        return m_new, alpha * acc + pv

    def body(j, c):
        s_cur, states = c
        new = []
        for h in range(nhead):
            s_next = scores(j, h + 1) if h + 1 < nhead else scores(j + 1, 0)
            new.append(absorb(j, h, s_cur, states[h], False))
            s_cur = s_next
        return s_cur, tuple(new)

    assert tq == tk
    init = tuple((jnp.full((1, tq), NEG_BIG, F32), jnp.zeros((hd + nsum, tq), F32)) for _ in range(nhead))
    s_cur, states = lax.fori_loop(0, i, body, (scores(0, 0), init))
    outs = []
    for h in range(nhead):
        s_next = scores(i, h + 1) if h + 1 < nhead else None
        _, acc = absorb(i, h, s_cur, states[h], True)
        outs.append(acc[0:hd] / acc[hd:hd + 1])
        s_cur = s_next
    o_ref[0] = jnp.concatenate(outs, axis=0).T


def _attn_prompt(qT, ka, vT, *, tq, tk, hd, npair):
    b, att, t = qT.shape
    width = 2 * hd * npair
    return pl.pallas_call(
        functools.partial(_attn_prompt_kernel, tq=tq, tk=tk, hd=hd, npair=npair),
        out_shape=jax.ShapeDtypeStruct((b, t, att), F32),
        grid=(b, att // width, t // tq),
        in_specs=[pl.BlockSpec((1, width, tq), lambda bi, hg, i: (bi, hg, i)),
                  pl.BlockSpec((1, t, 2 * width), lambda bi, hg, i: (bi, 0, hg)),
                  pl.BlockSpec((1, width, t), lambda bi, hg, i: (bi, hg, 0))],
        out_specs=pl.BlockSpec((1, tq, width), lambda bi, hg, i: (bi, i, hg)),
        compiler_params=_cparams(("arbitrary", "arbitrary", "arbitrary")),
        name="attn_prompt",
    )(qT, ka, vT)


def _cumsum_lanes(x):
    n = x.shape[1]
    lane = lax.broadcasted_iota(jnp.int32, x.shape, 1)
    d = 1
    while d < n:
        x = x + jnp.where(lane >= d, pltpu.roll(x, d, axis=1), 0.0)
        d *= 2
    return x


def _rows_as_lanes(a, height):
    r = a.shape[0]
    sq = jnp.concatenate([a, jnp.ones((LANES - r, LANES), F32)], axis=0).T
    return jnp.concatenate([sq] * (height // LANES), axis=0)


def _attn_sample_kernel(pt_ref, ck_hbm, cv_hbm, clf_hbm, q_ref, ks_ref, vs_ref, lfs_ref, o_ref,
                        kbuf, vbuf, lfbuf, sem, *, npg, ngroups, nchain, ts, hd, nh):
    b = pl.program_id(0)
    nb = pl.num_programs(0)
    att = nh * hd
    nrow = ts * nh
    page = kbuf.shape[3]
    assert ngroups % 2 == 0

    def group_copies(bb, g):
        slot = g % 2
        cps = []
        for ii in range(npg):
            pg = pt_ref[bb, g * npg + ii]
            cps.append(pltpu.make_async_copy(ck_hbm.at[pg], kbuf.at[slot, ii], sem.at[0, slot]))
            cps.append(pltpu.make_async_copy(cv_hbm.at[pg], vbuf.at[slot, ii], sem.at[1, slot]))
            cps.append(pltpu.make_async_copy(clf_hbm.at[pg], lfbuf.at[slot, ii], sem.at[2, slot]))
        return cps

    @pl.when(b == 0)
    def _():
        for cp in group_copies(b, 0):
            cp.start()

    q = q_ref[0]
    col_head = lax.broadcasted_iota(jnp.int32, (nh, att), 1) // hd
    row_head = lax.broadcasted_iota(jnp.int32, (nh, att), 0)
    blocks = [jnp.where(col_head == row_head, jnp.broadcast_to(q[tt:tt + 1, :], (nh, att)), 0.0)
              for tt in range(ts)]
    qp = jnp.concatenate(blocks, axis=0).astype(BF16)

    def update(state, k_t, v_t, bias, valid):
        m_old, l_old, acc = state
        n = k_t.shape[1]
        s = jnp.dot(qp, k_t, preferred_element_type=F32) + bias
        if valid is not None:
            s = jnp.where(valid, s, NEG_BIG)
        m_new = jnp.maximum(m_old, jnp.max(s, axis=1, keepdims=True))
        p = jnp.exp(s - jnp.concatenate([m_new] * (n // LANES), axis=1))
        alpha = jnp.exp(m_old - m_new)
        l_new = alpha * l_old + jnp.sum(p, axis=1, keepdims=True)
        p_pad = jnp.concatenate([p.astype(BF16), jnp.zeros((LANES - nrow, n), BF16)], axis=0)
        pv = lax.dot_general(v_t, p_pad, (((1,), (1,)), ((), ())), preferred_element_type=F32)
        return m_new, l_new, acc * _rows_as_lanes(alpha, att) + pv

    states = [(jnp.full((nrow, LANES), NEG_BIG, F32), jnp.zeros((nrow, LANES), F32), jnp.zeros((att, LANES), F32))
              for _ in range(nchain)]
    per_chain = npg // nchain
    off = jnp.zeros((nh, LANES), F32)
    for g in range(ngroups):
        slot = g % 2
        for cp in group_copies(b, g):
            cp.wait()
        if g + 1 < ngroups:
            for cp in group_copies(b, g + 1):
                cp.start()
        else:
            @pl.when(b + 1 < nb)
            def _():
                for cp in group_copies(b + 1, 0):
                    cp.start()
        lf = jnp.concatenate([lfbuf[slot, ii] for ii in range(npg)], axis=0)
        cum = _cumsum_lanes(lf)
        pieces = []
        for pi in range(npg):
            cp_ = cum[pi * nh:(pi + 1) * nh, :]
            pieces.append(jnp.concatenate([-(cp_ + off)] * ts, axis=0))
            off = off + jnp.broadcast_to(cp_[:, page - 1:page], (nh, LANES))
        for c in range(nchain):
            ids = range(c * per_chain, (c + 1) * per_chain)
            k_t = jnp.concatenate([kbuf[slot, ii] for ii in ids], axis=1).astype(BF16)
            v_t = jnp.concatenate([vbuf[slot, ii] for ii in ids], axis=1).astype(BF16)
            states[c] = update(states[c], k_t, v_t, jnp.concatenate([pieces[ii] for ii in ids], axis=1), None)

    zpad = jnp.zeros((LANES - ks_ref.shape[1], att), F32)
    ks_t = jnp.concatenate([ks_ref[0], zpad], axis=0).T.astype(BF16)
    vs_t = jnp.concatenate([vs_ref[0], zpad], axis=0).T.astype(BF16)
    fnew = _cumsum_lanes(lfs_ref[0]) + off
    bias = jnp.concatenate([-fnew] * ts, axis=0)
    key_t = lax.broadcasted_iota(jnp.int32, (nrow, LANES), 1)
    qry_t = lax.broadcasted_iota(jnp.int32, (nrow, LANES), 0) // nh
    valid = (key_t <= qry_t) & (key_t < ts)
    states[0] = update(states[0], ks_t, vs_t, bias, valid)
    m_all = states[0][0]
    for c in range(1, nchain):
        m_all = jnp.maximum(m_all, states[c][0])
    l_fin = jnp.zeros((nrow, LANES), F32)
    acc = jnp.zeros((att, LANES), F32)
    for c in range(nchain):
        wgt = jnp.exp(states[c][0] - m_all)
        l_fin = l_fin + wgt * states[c][1]
        acc = acc + states[c][2] * _rows_as_lanes(wgt, att)
    o_all = acc / _rows_as_lanes(l_fin, att)
    r_head = lax.broadcasted_iota(jnp.int32, (att, LANES), 0) // hd
    c_head = jnp.bitwise_and(lax.broadcasted_iota(jnp.int32, (att, LANES), 1), nh - 1)
    o_t = jnp.where(r_head == c_head, o_all, 0.0).T[0:nrow, :]
    o_ref[0] = jnp.sum(o_t.reshape(ts, nh, att), axis=1)


def _attn_sample(page_table, ck_t, cv_t, clf_t, q, ks, vs, lfs_t, *, npg, nchain, hd, nh):
    bs, ts, att = q.shape
    page = ck_t.shape[2]
    assert page == LANES
    n_pages = page_table.shape[1]
    ngroups = n_pages // npg
    tpad = ks.shape[1]

    per_b = lambda b, pt: (b, 0, 0)
    any_spec = pl.BlockSpec(memory_space=pl.ANY)
    grid_spec = pltpu.PrefetchScalarGridSpec(
        num_scalar_prefetch=1,
        grid=(bs,),
        in_specs=[any_spec, any_spec, any_spec,
                  pl.BlockSpec((1, ts, att), per_b), pl.BlockSpec((1, tpad, att), per_b),
                  pl.BlockSpec((1, tpad, att), per_b), pl.BlockSpec((1, nh, LANES), per_b)],
        out_specs=pl.BlockSpec((1, ts, att), per_b),
        scratch_shapes=[pltpu.VMEM((2, npg, att, page), F32), pltpu.VMEM((2, npg, att, page), F32),
                        pltpu.VMEM((2, npg, nh, page), F32), pltpu.SemaphoreType.DMA((3, 2))])
    return pl.pallas_call(
        functools.partial(_attn_sample_kernel, npg=npg, ngroups=ngroups, nchain=nchain, ts=ts, hd=hd, nh=nh),
        out_shape=jax.ShapeDtypeStruct((bs, ts, att), F32),
        grid_spec=grid_spec,
        compiler_params=_cparams(("arbitrary",)),
        name="attn_sample",
    )(page_table, ck_t, cv_t, clf_t, q, ks, vs, lfs_t)


def _merge_ffn_kernel(x_ref, rgn_ref, attn_ref, g1_ref, sh_ref, sc_ref, g2_ref, na_ref, nf_ref,
                      wo_ref, wg_ref, wu_ref, wd_ref, o_ref):
    attn_n = (_rms_rows(attn_ref[...]) * na_ref[...]).astype(BF16)
    mix_in = jnp.concatenate([rgn_ref[...], attn_n], axis=1)
    mix = jnp.dot(mix_in, wo_ref[...], preferred_element_type=F32)
    x1 = x_ref[...] + (1.0 + g1_ref[0]) * mix
    xn = _rms_rows(x1) * nf_ref[...]
    xn = (xn * (1.0 + sc_ref[0]) + sh_ref[0]).astype(BF16)
    gate = jnp.dot(xn, wg_ref[...], preferred_element_type=F32)
    up = jnp.dot(xn, wu_ref[...], preferred_element_type=F32)
    hid = (gate * jax.nn.sigmoid(gate) * up).astype(BF16)
    f = jnp.dot(hid, wd_ref[...], preferred_element_type=F32)
    o_ref[...] = x1 + (1.0 + g2_ref[0]) * f


def _merge_ffn(x2d, rgn, attn, mods4, na, nf, wo, wg, wu, wd, *, tm, rows_per_group, mod_rows):
    n, d = x2d.shape
    rg = rgn.shape[1]
    att = attn.shape[1]
    steps_per_group = rows_per_group // tm
    row = lambda i: (i, 0)
    const = lambda i: (0, 0)
    if mod_rows == 1:
        mod_spec = pl.BlockSpec((1, 1, d), lambda i: (i // steps_per_group, 0, 0))
    else:
        mod_spec = pl.BlockSpec((1, tm, d), lambda i: (0, i, 0))
    wspec = lambda w: pl.BlockSpec(w.shape, const, pipeline_mode=pl.Buffered(1))
    return pl.pallas_call(
        _merge_ffn_kernel,
        out_shape=jax.ShapeDtypeStruct((n, d), F32),
        grid=(n // tm,),
        in_specs=[pl.BlockSpec((tm, d), row), pl.BlockSpec((tm, rg), row), pl.BlockSpec((tm, att), row),
                  mod_spec, mod_spec, mod_spec, mod_spec,
                  pl.BlockSpec((1, att), const), pl.BlockSpec((1, d), const),
                  wspec(wo), wspec(wg), wspec(wu), wspec(wd)],
        out_specs=pl.BlockSpec((tm, d), row),
        compiler_params=_cparams(("arbitrary",)),
        name="merge_ffn",
    )(x2d, rgn, attn, *mods4, na, nf, wo, wg, wu, wd)


def _pair_blockdiag(wa, wx):
    nb, c, _ = wa.shape
    z = jnp.zeros((nb // 2, c, c), wa.dtype)

    def pair(w):
        w = w.reshape(nb // 2, 2, c, c)
        top = jnp.concatenate([w[:, 0], z], axis=2)
        bot = jnp.concatenate([z, w[:, 1]], axis=2)
        return jnp.concatenate([top, bot], axis=1)

    return jnp.concatenate([pair(wa), pair(wx)], axis=2).astype(BF16)


def kernel(x_prompt, x_sample, cache_k, cache_v, cache_logf, state_conv, state_h, page_table,
           c_prompt, c_sample, norm_mix, w_ada, b_ada, w_in, b_f, g_q, g_k, w_conv, b_conv,
           w_a, b_a, w_x, b_x, lam, norm_rg, norm_attn, w_out, norm_ffn,
           w_ffn_gate, w_ffn_up, w_ffn_down):
    depth = w_in.shape[0]
    assert depth == 1, "single-layer step"
    b, t, d = x_prompt.shape
    bs, ts, _ = x_sample.shape
    nh, hd = cache_k.shape[3], cache_k.shape[4]
    att = nh * hd
    rg = state_h.shape[-1]
    n_pool, page = cache_k.shape[1], cache_k.shape[2]
    in_cols = w_in.shape[2]
    assert in_cols == 2 * rg + 3 * att + nh

    tm_in = 512
    tm_rg = 512
    tq = 512
    tk = 512
    tm_ffn = 256
    npg = 32
    nchain = 2

    wcols = -(-in_cols // LANES) * LANES
    w_in_b = jnp.pad(w_in[0], ((0, 0), (0, wcols - in_cols))).astype(BF16)
    ones_heads = jnp.kron(jnp.eye(nh, dtype=F32), jnp.ones((hd, hd), F32)).astype(BF16)
    gq_row = jnp.tile(g_q[0], nh)[None, :] * (hd ** -0.5)
    gq_row2 = gq_row * LOG2E
    gk_row = jnp.tile(g_k[0], nh)[None, :]
    assert 2 * hd == LANES and 3 * nh <= LANES
    sel_np = [[0.0] * att for _ in range(LANES)]
    for h in range(nh):
        for kk in range(3):
            sel_np[kk * nh + h][LANES * (h // 2) + 3 * (h % 2) + kk] = 1.0
    sel_terms = jnp.array(sel_np, BF16)
    wa_pairs = _pair_blockdiag(w_a[0], w_x[0])
    wo_b = w_out[0].astype(BF16)
    wg_b = w_ffn_gate[0].astype(BF16)
    wu_b = w_ffn_up[0].astype(BF16)
    wd_b = w_ffn_down[0].astype(BF16)
    row = lambda a: a[0][None, :]

    mods = _ada(jnp.concatenate([c_prompt, c_sample], axis=0), w_ada[0], b_ada[0][None, :])
    mods = mods.reshape(b + bs, 6, d)
    mods_p = [mods[:b, i:i + 1] for i in range(6)]
    mods_s = [jnp.repeat(mods[b:, i], ts, axis=0)[None] for i in range(6)]

    xp2d = x_prompt.reshape(b * t, d)
    (xr, gr, knT, ka, vT32, qT, vT, logf) = _inproj_prompt(
        xp2d, mods_p[0], mods_p[1], row(norm_mix), w_in_b, ones_heads, gq_row2, gk_row, row(b_f), sel_terms,
        b=b, t=t, tm=tm_in, rg=rg, att=att, hd=hd, nh=nh)
    rgn, h_last = _rg_prompt(xr, gr, w_conv[0], row(b_conv), wa_pairs, row(b_a), row(b_x), row(lam),
                             row(norm_rg), b=b, t=t, tm=tm_rg)
    attn = _attn_prompt(qT, ka.reshape(b, t, 2 * att), vT, tq=tq, tk=tk, hd=hd, npair=2)
    y_prompt = _merge_ffn(xp2d, rgn, attn.reshape(b * t, att), [mods_p[2], mods_p[3], mods_p[4], mods_p[5]],
                          row(norm_attn), row(norm_ffn), wo_b, wg_b, wu_b, wd_b,
                          tm=tm_ffn, rows_per_group=t, mod_rows=1)

    ns = bs * ts
    xs2d = x_sample.reshape(ns, d)
    (xr_s, gr_s, q_s, kn_s, v_s, logf_s) = _inproj_sample(
        xs2d, mods_s[0], mods_s[1], row(norm_mix), w_in_b, ones_heads, gq_row, gk_row, row(b_f),
        rg=rg, att=att, hd=hd, nh=nh)
    x_ext = jnp.concatenate([state_conv[0], xr_s.reshape(bs, ts, rg)], axis=1)
    xwin = [x_ext[:, j:j + ts].reshape(ns, rg) for j in range(CONV_WIDTH)]
    h0_rows = jnp.repeat(state_h[0], ts, axis=0)
    rgn_s, h_rows = _rg_sample(xwin, gr_s, h0_rows, w_conv[0], row(b_conv), wa_pairs, row(b_a), row(b_x),
                               row(lam), row(norm_rg), seg=ts)
    tpad = SUBLANES
    pad_t = lambda a: jnp.pad(a, ((0, 0), (0, tpad - ts), (0, 0)))
    ck_t = jnp.transpose(cache_k[0], (0, 2, 3, 1)).reshape(n_pool, att, page)
    cv_t = jnp.transpose(cache_v[0], (0, 2, 3, 1)).reshape(n_pool, att, page)
    clf_t = jnp.transpose(cache_logf[0], (0, 2, 1))
    lfs_t = jnp.pad(jnp.transpose(logf_s.reshape(bs, ts, nh), (0, 2, 1)), ((0, 0), (0, 0), (0, LANES - ts)))
    attn_s = _attn_sample(
        page_table, ck_t, cv_t, clf_t, q_s.reshape(bs, ts, att), pad_t(kn_s.reshape(bs, ts, att)),
        pad_t(v_s.reshape(bs, ts, att)), lfs_t, npg=npg, nchain=nchain, hd=hd, nh=nh)
    y_sample = _merge_ffn(xs2d, rgn_s, attn_s.reshape(ns, att), [mods_s[2], mods_s[3], mods_s[4], mods_s[5]],
                          row(norm_attn), row(norm_ffn), wo_b, wg_b, wu_b, wd_b,
                          tm=min(tm_ffn, ns), rows_per_group=ns, mod_rows=min(tm_ffn, ns))

    nc = CONV_WIDTH - 1
    heads_last = lambda a: jnp.transpose(a.reshape(b, nh, hd, t), (0, 3, 1, 2))[None]
    return (y_prompt.reshape(b, t, d), y_sample.reshape(bs, ts, d),
            heads_last(knT), heads_last(vT32), logf.reshape(1, b, t, nh),
            xr.reshape(b, t, rg)[:, t - nc:, :][None], h_last.reshape(1, b, rg),
            kn_s.reshape(1, bs, ts, nh, hd), v_s.reshape(1, bs, ts, nh, hd), logf_s.reshape(1, bs, ts, nh),
            x_ext[:, ts:, :][None], h_rows.reshape(bs, ts, rg)[:, ts - 1, :][None])
```

```python
import functools

import jax
import jax.numpy as jnp
from jax import lax
from jax.experimental import pallas as pl
from jax.experimental.pallas import tpu as pltpu

EPS = 1e-6
RG_C = 8.0
CONV_WIDTH = 4
LANES = 128
SUBLANES = 8
NEG_BIG = -1e30
LOG2E = 1.4426950408889634
VMEM_LIMIT = 56 * 1024 * 1024

F32 = jnp.float32
BF16 = jnp.bfloat16


def _cparams(sem, vmem=VMEM_LIMIT):
    return pltpu.CompilerParams(dimension_semantics=sem, vmem_limit_bytes=vmem)


def _rms_rows(x):
    return x * lax.rsqrt(jnp.mean(x * x, axis=-1, keepdims=True) + EPS)


def _log_sigmoid(x):
    return -(jnp.maximum(-x, 0.0) + jnp.log1p(jnp.exp(-jnp.abs(x))))


def _cumsum_rows(x, seg=None):
    n = x.shape[0]
    seg = n if seg is None else seg
    row = lax.broadcasted_iota(jnp.int32, x.shape, 0)
    pos = row if seg == n else jnp.bitwise_and(row, seg - 1)
    d = 1
    while d < seg:
        x = x + jnp.where(pos >= d, pltpu.roll(x, d, axis=0), 0.0)
        d *= 2
    return x


def _scan_rows(a, u, seg=None):
    n = a.shape[0]
    seg = n if seg is None else seg
    row = lax.broadcasted_iota(jnp.int32, a.shape, 0)
    pos = row if seg == n else jnp.bitwise_and(row, seg - 1)
    d = 1
    while d < seg:
        m = pos >= d
        a_s = jnp.where(m, pltpu.roll(a, d, axis=0), 1.0)
        u_s = jnp.where(m, pltpu.roll(u, d, axis=0), 0.0)
        u = u + a * u_s
        a = a * a_s
        d *= 2
    return a, u


def _ada_kernel(c_ref, w_ref, b_ref, o_ref):
    c = c_ref[...]
    s = (c * jax.nn.sigmoid(c)).astype(BF16)
    o_ref[...] = jnp.dot(s, w_ref[...].astype(BF16), preferred_element_type=F32) + b_ref[...]


def _ada(c, w, b):
    m, d = c.shape
    n = w.shape[1]
    tn = n // 4
    return pl.pallas_call(
        _ada_kernel,
        out_shape=jax.ShapeDtypeStruct((m, n), F32),
        grid=(n // tn,),
        in_specs=[pl.BlockSpec((m, d), lambda j: (0, 0)),
                  pl.BlockSpec((d, tn), lambda j: (0, j)),
                  pl.BlockSpec((1, tn), lambda j: (0, j))],
        out_specs=pl.BlockSpec((m, tn), lambda j: (0, j)),
        compiler_params=_cparams(("arbitrary",)),
        name="ada_mod",
    )(c, w, b)


def _head_rms(t, ones_bf16, gain_row, hd):
    ssq = jnp.dot((t * t).astype(BF16), ones_bf16, preferred_element_type=F32)
    return t * lax.rsqrt(ssq * (1.0 / hd) + EPS) * gain_row


def _inproj_body(x_ref, shift_ref, scale_ref, g_ref, w_ref, ones_ref, gq_ref, gk_ref, bf_ref, rg, att, hd, nh):
    x = x_ref[...]
    xn = _rms_rows(x) * g_ref[...]
    xn = xn * (1.0 + scale_ref[0]) + shift_ref[0]
    z = jnp.dot(xn.astype(BF16), w_ref[...], preferred_element_type=F32)
    xr = z[:, 0:rg]
    gr = z[:, rg:2 * rg]
    q = z[:, 2 * rg:2 * rg + att]
    k = z[:, 2 * rg + att:2 * rg + 2 * att]
    v = z[:, 2 * rg + 2 * att:2 * rg + 3 * att]
    fl = z[:, 2 * rg + 3 * att:2 * rg + 3 * att + nh]
    ones = ones_ref[...]
    qn = _head_rms(q, ones, gq_ref[...], hd)
    kn = _head_rms(k, ones, gk_ref[...], hd)
    logf = _log_sigmoid(fl + bf_ref[...])
    return xr, gr, qn, kn, v, logf


def _inproj_prompt_kernel(x_ref, shift_ref, scale_ref, g_ref, w_ref, ones_ref, gq_ref, gk_ref, bf_ref, sel_ref,
                          xr_ref, gr_ref, knT_ref, ka_ref, vT32_ref, qT_ref, vT_ref, logf_ref,
                          carry_ref, *, rg, att, hd, nh):
    t = pl.program_id(1)

    @pl.when(t == 0)
    def _():
        carry_ref[...] = jnp.zeros_like(carry_ref)

    xr, gr, qn, kn, v, logf = _inproj_body(x_ref, shift_ref, scale_ref, g_ref, w_ref, ones_ref,
                                           gq_ref, gk_ref, bf_ref, rg, att, hd, nh)
    xr_ref[...] = xr
    gr_ref[...] = gr
    knT_ref[0] = kn.T
    v_t = v.T
    vT32_ref[0] = v_t
    qT_ref[0] = qn.T.astype(BF16)
    vT_ref[0] = v_t.astype(BF16)
    logf_ref[...] = logf
    fc = _cumsum_rows(logf) + carry_ref[0:1, 0:nh]
    n = fc.shape[0]
    carry_ref[0:1, 0:nh] = fc[n - 1:n, :]
    fb = fc * (-LOG2E)
    hi = fb.astype(BF16).astype(F32)
    r1 = fb - hi
    mid = r1.astype(BF16).astype(F32)
    lo = r1 - mid
    terms = jnp.concatenate([hi, mid, lo, jnp.zeros((n, LANES - 3 * nh), F32)], axis=1).astype(BF16)
    aug = jnp.dot(terms, sel_ref[...], preferred_element_type=F32).astype(BF16)
    kb = kn.astype(BF16)
    parts = []
    for hp in range(att // LANES):
        parts += [kb[:, hp * LANES:(hp + 1) * LANES], aug[:, hp * LANES:(hp + 1) * LANES]]
    ka_ref[...] = jnp.concatenate(parts, axis=1)


def _inproj_sample_kernel(x_ref, shift_ref, scale_ref, g_ref, w_ref, ones_ref, gq_ref, gk_ref, bf_ref,
                          xr_ref, gr_ref, q_ref, kn_ref, v_ref, logf_ref, *, rg, att, hd, nh):
    xr, gr, qn, kn, v, logf = _inproj_body(x_ref, shift_ref, scale_ref, g_ref, w_ref, ones_ref,
                                           gq_ref, gk_ref, bf_ref, rg, att, hd, nh)
    xr_ref[...] = xr
    gr_ref[...] = gr
    q_ref[...] = qn
    kn_ref[...] = kn
    v_ref[...] = v
    logf_ref[...] = logf


def _inproj_common_specs(d, wcols, att, nh, mod_rows, mod_map):
    const = lambda *_: (0, 0)
    return [pl.BlockSpec((1, mod_rows, d), mod_map),
            pl.BlockSpec((1, mod_rows, d), mod_map),
            pl.BlockSpec((1, d), const),
            pl.BlockSpec((d, wcols), const),
            pl.BlockSpec((att, att), const),
            pl.BlockSpec((1, att), const),
            pl.BlockSpec((1, att), const),
            pl.BlockSpec((1, nh), const)]


def _inproj_prompt(x2d, shift, scale, g, w, ones, gq, gk, bf, sel, *, b, t, tm, rg, att, hd, nh):
    n, d = x2d.shape
    nt = t // tm
    row = lambda bi, ti: (bi * nt + ti, 0)
    tr = lambda bi, ti: (bi, 0, ti)
    kern = functools.partial(_inproj_prompt_kernel, rg=rg, att=att, hd=hd, nh=nh)
    return pl.pallas_call(
        kern,
        out_shape=(jax.ShapeDtypeStruct((n, rg), F32), jax.ShapeDtypeStruct((n, rg), F32),
                   jax.ShapeDtypeStruct((b, att, t), F32), jax.ShapeDtypeStruct((n, 2 * att), BF16),
                   jax.ShapeDtypeStruct((b, att, t), F32),
                   jax.ShapeDtypeStruct((b, att, t), BF16), jax.ShapeDtypeStruct((b, att, t), BF16),
                   jax.ShapeDtypeStruct((n, nh), F32)),
        grid=(b, nt),
        in_specs=[pl.BlockSpec((tm, d), row)] + _inproj_common_specs(
            d, w.shape[1], att, nh, 1, lambda bi, ti: (bi, 0, 0)) + [pl.BlockSpec(sel.shape, lambda bi, ti: (0, 0))],
        out_specs=(pl.BlockSpec((tm, rg), row), pl.BlockSpec((tm, rg), row),
                   pl.BlockSpec((1, att, tm), tr), pl.BlockSpec((tm, 2 * att), row),
                   pl.BlockSpec((1, att, tm), tr),
                   pl.BlockSpec((1, att, tm), tr), pl.BlockSpec((1, att, tm), tr),
                   pl.BlockSpec((tm, nh), row)),
        scratch_shapes=[pltpu.VMEM((SUBLANES, LANES), F32)],
        compiler_params=_cparams(("arbitrary", "arbitrary")),
        name="inproj_prompt",
    )(x2d, shift, scale, g, w, ones, gq, gk, bf, sel)


def _inproj_sample(x2d, shift, scale, g, w, ones, gq, gk, bf, *, rg, att, hd, nh):
    n, d = x2d.shape
    row = lambda i: (0, 0)
    kern = functools.partial(_inproj_sample_kernel, rg=rg, att=att, hd=hd, nh=nh)
    return pl.pallas_call(
        kern,
        out_shape=(jax.ShapeDtypeStruct((n, rg), F32), jax.ShapeDtypeStruct((n, rg), F32),
                   jax.ShapeDtypeStruct((n, att), F32), jax.ShapeDtypeStruct((n, att), F32),
                   jax.ShapeDtypeStruct((n, att), F32), jax.ShapeDtypeStruct((n, nh), F32)),
        grid=(1,),
        in_specs=[pl.BlockSpec((n, d), row)] + _inproj_common_specs(
            d, w.shape[1], att, nh, n, lambda i: (0, 0, 0)),
        out_specs=(pl.BlockSpec((n, rg), row), pl.BlockSpec((n, rg), row),
                   pl.BlockSpec((n, att), row), pl.BlockSpec((n, att), row),
                   pl.BlockSpec((n, att), row), pl.BlockSpec((n, nh), row)),
        compiler_params=_cparams(("arbitrary",)),
        name="inproj_sample",
    )(x2d, shift, scale, g, w, ones, gq, gk, bf)


def _rg_gates(xc, wa_ref, ba, bx, lam):
    sp = jnp.maximum(-lam, 0.0) + jnp.log1p(jnp.exp(-jnp.abs(lam)))
    a_parts, u_parts = [], []
    npair = wa_ref.shape[0]
    pw = wa_ref.shape[1]
    for j in range(npair):
        sl = slice(pw * j, pw * (j + 1))
        xcj = xc[:, sl]
        g = jnp.dot(xcj.astype(BF16), wa_ref[j], preferred_element_type=F32)
        r = jax.nn.sigmoid(g[:, :pw] + ba[:, sl])
        i = jax.nn.sigmoid(g[:, pw:] + bx[:, sl])
        log_a = (-RG_C) * r * sp[:, sl]
        a = jnp.exp(log_a)
        a_parts.append(a)
        w2 = -jnp.tanh(log_a) * (a * a + 1.0)
        w = jnp.where(w2 > 0.0, w2 * lax.rsqrt(w2), 0.0)
        u_parts.append(w * (i * xcj))
    return a_parts, u_parts


def _rg_finish(h, gr, norm_row):
    out = h * jax.nn.gelu(gr)
    return (_rms_rows(out) * norm_row).astype(BF16)


def _rg_prompt_kernel(xr_ref, gr_ref, wc_ref, bc_ref, wa_ref, ba_ref, bx_ref, lam_ref, nrm_ref,
                      o_ref, hT_ref, xbuf_ref, h_ref, sa_ref, su_ref, *, tm):
    t = pl.program_id(1)

    @pl.when(t == 0)
    def _():
        xbuf_ref[0:SUBLANES, :] = jnp.zeros((SUBLANES, xbuf_ref.shape[1]), F32)
        h_ref[...] = jnp.zeros_like(h_ref)

    xbuf_ref[SUBLANES:SUBLANES + tm, :] = xr_ref[...]
    xfull = xbuf_ref[...]
    xc = bc_ref[...] + xfull[SUBLANES:, :] * wc_ref[CONV_WIDTH - 1:CONV_WIDTH, :]
    for dly in range(1, CONV_WIDTH):
        tap = CONV_WIDTH - 1 - dly
        xc = xc + pltpu.roll(xfull, dly, axis=0)[SUBLANES:, :] * wc_ref[tap:tap + 1, :]
    xbuf_ref[0:SUBLANES, :] = xbuf_ref[tm:tm + SUBLANES, :]

    a_parts, u_parts = _rg_gates(xc, wa_ref, ba_ref[...], bx_ref[...], lam_ref[...])
    pw = wa_ref.shape[1]
    for j, (a, u) in enumerate(zip(a_parts, u_parts)):
        big_a, big_u = _scan_rows(a, u, seg=SUBLANES)
        sa_ref[:, pw * j:pw * (j + 1)] = big_a
        su_ref[:, pw * j:pw * (j + 1)] = big_u

    def group(g, h_prev):
        r0 = pl.multiple_of(g * SUBLANES, SUBLANES)
        hg = sa_ref[pl.ds(r0, SUBLANES), :] * h_prev + su_ref[pl.ds(r0, SUBLANES), :]
        su_ref[pl.ds(r0, SUBLANES), :] = hg
        return hg[SUBLANES - 1:SUBLANES, :]

    h_last = lax.fori_loop(0, tm // SUBLANES, group, h_ref[0:1, :], unroll=8)
    h = su_ref[...]
    h_ref[0:1, :] = h_last
    hT_ref[0] = h_last
    o_ref[...] = _rg_finish(h, gr_ref[...], nrm_ref[...])


def _rg_prompt(xr, gr, wc, bc, wa, ba, bx, lam, nrm, *, b, t, tm):
    n, rg = xr.shape
    nt = t // tm
    row = lambda bi, ti: (bi * nt + ti, 0)
    const2 = lambda bi, ti: (0, 0)
    return pl.pallas_call(
        functools.partial(_rg_prompt_kernel, tm=tm),
        out_shape=(jax.ShapeDtypeStruct((n, rg), BF16), jax.ShapeDtypeStruct((b, 1, rg), F32)),
        grid=(b, nt),
        in_specs=[pl.BlockSpec((tm, rg), row), pl.BlockSpec((tm, rg), row),
                  pl.BlockSpec(wc.shape, const2), pl.BlockSpec((1, rg), const2),
                  pl.BlockSpec(wa.shape, lambda bi, ti: (0, 0, 0)),
                  pl.BlockSpec((1, rg), const2), pl.BlockSpec((1, rg), const2),
                  pl.BlockSpec((1, rg), const2), pl.BlockSpec((1, rg), const2)],
        out_specs=(pl.BlockSpec((tm, rg), row), pl.BlockSpec((1, 1, rg), lambda bi, ti: (bi, 0, 0))),
        scratch_shapes=[pltpu.VMEM((tm + SUBLANES, rg), F32), pltpu.VMEM((SUBLANES, rg), F32),
                        pltpu.VMEM((tm, rg), F32), pltpu.VMEM((tm, rg), F32)],
        compiler_params=_cparams(("arbitrary", "arbitrary")),
        name="rg_prompt",
    )(xr, gr, wc, bc, wa, ba, bx, lam, nrm)


def _rg_sample_kernel(x0_ref, x1_ref, x2_ref, x3_ref, gr_ref, h0_ref, wc_ref, bc_ref, wa_ref, ba_ref, bx_ref,
                      lam_ref, nrm_ref, o_ref, h_ref, *, seg):
    xs = (x0_ref, x1_ref, x2_ref, x3_ref)
    xc = bc_ref[...]
    for j in range(CONV_WIDTH):
        xc = xc + xs[j][...] * wc_ref[j:j + 1, :]
    a_parts, u_parts = _rg_gates(xc, wa_ref, ba_ref[...], bx_ref[...], lam_ref[...])
    pw = wa_ref.shape[1]
    h_parts = []
    for j, (a, u) in enumerate(zip(a_parts, u_parts)):
        big_a, big_u = _scan_rows(a, u, seg=seg)
        h_parts.append(big_a * h0_ref[:, pw * j:pw * (j + 1)] + big_u)
    h = jnp.concatenate(h_parts, axis=1)
    h_ref[...] = h
    o_ref[...] = _rg_finish(h, gr_ref[...], nrm_ref[...])


def _rg_sample(xwin, gr, h0_rows, wc, bc, wa, ba, bx, lam, nrm, *, seg):
    n, rg = gr.shape
    full = pl.BlockSpec((n, rg), lambda i: (0, 0))
    vec = pl.BlockSpec((1, rg), lambda i: (0, 0))
    return pl.pallas_call(
        functools.partial(_rg_sample_kernel, seg=seg),
        out_shape=(jax.ShapeDtypeStruct((n, rg), BF16), jax.ShapeDtypeStruct((n, rg), F32)),
        grid=(1,),
        in_specs=[full, full, full, full, full, full,
                  pl.BlockSpec(wc.shape, lambda i: (0, 0)), vec,
                  pl.BlockSpec(wa.shape, lambda i: (0, 0, 0)), vec, vec, vec, vec],
        out_specs=(full, full),
        compiler_params=_cparams(("arbitrary",)),
        name="rg_sample",
    )(*xwin, gr, h0_rows, wc, bc, wa, ba, bx, lam, nrm)


def _attn_prompt_kernel(qT_ref, ka_ref, vT_ref, o_ref, *, tq, tk, hd, npair, nsplit, nunroll):
    i = pl.program_id(2)
    pair = 2 * hd
    nhead = 2 * npair
    nsum = 16

    qs = []
    for pp in range(npair):
        qp = qT_ref[0, pp * pair:(pp + 1) * pair, :]
        rowq = lax.broadcasted_iota(jnp.int32, qp.shape, 0)
        zero = jnp.zeros_like(qp)
        for hh in range(2):
            qh = jnp.where(rowq < hd, qp, zero) if hh == 0 else jnp.where(rowq >= hd, qp, zero)
            pick = (jnp.where(rowq < 3 * hh + 3, 1.0, 0.0) - jnp.where(rowq < 3 * hh, 1.0, 0.0)).astype(BF16)
            qs.append(jnp.concatenate([qh, pick], axis=0))
    ones_rows = jnp.ones((nsum, tk), BF16)

    wq = tq // nsplit
    items = [(h, c) for h in range(nhead) for c in range(nsplit)]

    def scores(j, it):
        h, c = it
        pp = h // 2
        start = pl.multiple_of(j * tk, tk)
        kblk = ka_ref[0, pl.ds(start, tk), 2 * pair * pp:2 * pair * (pp + 1)]
        return jnp.dot(kblk, qs[h][:, c * wq:(c + 1) * wq], preferred_element_type=F32)

    def absorb(j, it, s, state, masked):
        h, c = it
        m, acc = state
        start = pl.multiple_of(j * tk, tk)
        if masked:
            rk = lax.broadcasted_iota(jnp.int32, s.shape, 0)
            cq = lax.broadcasted_iota(jnp.int32, s.shape, 1) + c * wq
            s = jnp.where(rk <= cq, s, NEG_BIG)
        m_new = jnp.maximum(m, jnp.max(s, axis=0, keepdims=True))
        p = jnp.exp2(s - m_new).astype(BF16)
        alpha = jnp.exp2(m - m_new)
        vblk = jnp.concatenate([vT_ref[0, h * hd:(h + 1) * hd, pl.ds(start, tk)], ones_rows], axis=0)
        pv = jnp.dot(vblk, p, preferred_element_type=F32)
        return m_new, alpha * acc + pv

    def make_body(nblk):
        def body(j, c):
            s_cur, states = c
            states = list(states)
            for u in range(nblk):
                blk = j * nblk + u
                for n, it in enumerate(items):
                    s_next = scores(blk, items[n + 1]) if n + 1 < len(items) else scores(blk + 1, items[0])
                    states[n] = absorb(blk, it, s_cur, states[n], False)
                    s_cur = s_next
            return s_cur, tuple(states)
        return body

    assert tq == tk
    init = tuple((jnp.full((1, wq), NEG_BIG, F32), jnp.zeros((hd + nsum, wq), F32)) for _ in items)
    carry = lax.fori_loop(0, i // nunroll, make_body(nunroll), (scores(0, items[0]), init))
    s_cur, states = lax.fori_loop((i // nunroll) * nunroll, i, make_body(1), carry)
    outs = []
    for n, it in enumerate(items):
        s_next = scores(i, items[n + 1]) if n + 1 < len(items) else None
        _, acc = absorb(i, it, s_cur, states[n], True)
        outs.append(acc[0:hd] / acc[hd:hd + 1])
        s_cur = s_next
    rows = [jnp.concatenate(outs[h * nsplit:(h + 1) * nsplit], axis=1) for h in range(nhead)]
    o_ref[0] = jnp.concatenate(rows, axis=0).T


def _attn_prompt(qT, ka, vT, *, tq, tk, hd, npair, nsplit, nunroll):
    b, att, t = qT.shape
    width = 2 * hd * npair
    return pl.pallas_call(
        functools.partial(_attn_prompt_kernel, tq=tq, tk=tk, hd=hd, npair=npair, nsplit=nsplit,
                          nunroll=nunroll),
        out_shape=jax.ShapeDtypeStruct((b, t, att), F32),
        grid=(b, att // width, t // tq),
        in_specs=[pl.BlockSpec((1, width, tq), lambda bi, hg, i: (bi, hg, i)),
                  pl.BlockSpec((1, t, 2 * width), lambda bi, hg, i: (bi, 0, hg)),
                  pl.BlockSpec((1, width, t), lambda bi, hg, i: (bi, hg, 0))],
        out_specs=pl.BlockSpec((1, tq, width), lambda bi, hg, i: (bi, i, hg)),
        compiler_params=_cparams(("arbitrary", "arbitrary", "arbitrary")),
        name="attn_prompt",
    )(qT, ka, vT)


def _cumsum_lanes(x):
    n = x.shape[1]
    lane = lax.broadcasted_iota(jnp.int32, x.shape, 1)
    d = 1
    while d < n:
        x = x + jnp.where(lane >= d, pltpu.roll(x, d, axis=1), 0.0)
        d *= 2
    return x


def _rows_as_lanes(a, height):
    r = a.shape[0]
    sq = jnp.concatenate([a, jnp.ones((LANES - r, LANES), F32)], axis=0).T
    return jnp.concatenate([sq] * (height // LANES), axis=0)


def _attn_sample_kernel(pt_ref, ck_hbm, cv_hbm, clf_hbm, q_ref, ks_ref, vs_ref, lfs_ref, o_ref,
                        kbuf, vbuf, lfbuf, sem, *, npg, ngroups, nchain, ts, hd, nh):
    b = pl.program_id(0)
    nb = pl.num_programs(0)
    att = nh * hd
    nrow = ts * nh
    page = kbuf.shape[3]
    assert ngroups % 2 == 0

    def group_copies(bb, g):
        slot = g % 2
        cps = []
        for ii in range(npg):
            pg = pt_ref[bb, g * npg + ii]
            cps.append(pltpu.make_async_copy(ck_hbm.at[pg], kbuf.at[slot, ii], sem.at[0, slot]))
            cps.append(pltpu.make_async_copy(cv_hbm.at[pg], vbuf.at[slot, ii], sem.at[1, slot]))
            cps.append(pltpu.make_async_copy(clf_hbm.at[pg], lfbuf.at[slot, ii], sem.at[2, slot]))
        return cps

    @pl.when(b == 0)
    def _():
        for cp in group_copies(b, 0):
            cp.start()

    q = q_ref[0]
    col_head = lax.broadcasted_iota(jnp.int32, (nh, att), 1) // hd
    row_head = lax.broadcasted_iota(jnp.int32, (nh, att), 0)
    blocks = [jnp.where(col_head == row_head, jnp.broadcast_to(q[tt:tt + 1, :], (nh, att)), 0.0)
              for tt in range(ts)]
    qp = jnp.concatenate(blocks, axis=0).astype(BF16)

    def update(state, k_t, v_t, bias, valid):
        m_old, l_old, acc = state
        n = k_t.shape[1]
        s = jnp.dot(qp, k_t, preferred_element_type=F32) + bias
        if valid is not None:
            s = jnp.where(valid, s, NEG_BIG)
        m_new = jnp.maximum(m_old, jnp.max(s, axis=1, keepdims=True))
        p = jnp.exp(s - jnp.concatenate([m_new] * (n // LANES), axis=1))
        alpha = jnp.exp(m_old - m_new)
        l_new = alpha * l_old + jnp.sum(p, axis=1, keepdims=True)
        p_pad = jnp.concatenate([p.astype(BF16), jnp.zeros((LANES - nrow, n), BF16)], axis=0)
        pv = lax.dot_general(v_t, p_pad, (((1,), (1,)), ((), ())), preferred_element_type=F32)
        return m_new, l_new, acc * _rows_as_lanes(alpha, att) + pv

    states = [(jnp.full((nrow, LANES), NEG_BIG, F32), jnp.zeros((nrow, LANES), F32), jnp.zeros((att, LANES), F32))
              for _ in range(nchain)]
    per_chain = npg // nchain
    off = jnp.zeros((nh, LANES), F32)
    for g in range(ngroups):
        slot = g % 2
        for cp in group_copies(b, g):
            cp.wait()
        if g + 1 < ngroups:
            for cp in group_copies(b, g + 1):
                cp.start()
        else:
            @pl.when(b + 1 < nb)
            def _():
                for cp in group_copies(b + 1, 0):
                    cp.start()
        lf = jnp.concatenate([lfbuf[slot, ii] for ii in range(npg)], axis=0)
        cum = _cumsum_lanes(lf)
        pieces = []
        for pi in range(npg):
            cp_ = cum[pi * nh:(pi + 1) * nh, :]
            pieces.append(jnp.concatenate([-(cp_ + off)] * ts, axis=0))
            off = off + jnp.broadcast_to(cp_[:, page - 1:page], (nh, LANES))
        for c in range(nchain):
            ids = range(c * per_chain, (c + 1) * per_chain)
            k_t = jnp.concatenate([kbuf[slot, ii] for ii in ids], axis=1).astype(BF16)
            v_t = jnp.concatenate([vbuf[slot, ii] for ii in ids], axis=1).astype(BF16)
            states[c] = update(states[c], k_t, v_t, jnp.concatenate([pieces[ii] for ii in ids], axis=1), None)

    zpad = jnp.zeros((LANES - ks_ref.shape[1], att), F32)
    ks_t = jnp.concatenate([ks_ref[0], zpad], axis=0).T.astype(BF16)
    vs_t = jnp.concatenate([vs_ref[0], zpad], axis=0).T.astype(BF16)
    fnew = _cumsum_lanes(lfs_ref[0]) + off
    bias = jnp.concatenate([-fnew] * ts, axis=0)
    key_t = lax.broadcasted_iota(jnp.int32, (nrow, LANES), 1)
    qry_t = lax.broadcasted_iota(jnp.int32, (nrow, LANES), 0) // nh
    valid = (key_t <= qry_t) & (key_t < ts)
    states[0] = update(states[0], ks_t, vs_t, bias, valid)
    m_all = states[0][0]
    for c in range(1, nchain):
        m_all = jnp.maximum(m_all, states[c][0])
    l_fin = jnp.zeros((nrow, LANES), F32)
    acc = jnp.zeros((att, LANES), F32)
    for c in range(nchain):
        wgt = jnp.exp(states[c][0] - m_all)
        l_fin = l_fin + wgt * states[c][1]
        acc = acc + states[c][2] * _rows_as_lanes(wgt, att)
    o_all = acc / _rows_as_lanes(l_fin, att)
    r_head = lax.broadcasted_iota(jnp.int32, (att, LANES), 0) // hd
    c_head = jnp.bitwise_and(lax.broadcasted_iota(jnp.int32, (att, LANES), 1), nh - 1)
    o_t = jnp.where(r_head == c_head, o_all, 0.0).T[0:nrow, :]
    o_ref[0] = jnp.sum(o_t.reshape(ts, nh, att), axis=1)


def _attn_sample(page_table, ck_t, cv_t, clf_t, q, ks, vs, lfs_t, *, npg, nchain, hd, nh):
    bs, ts, att = q.shape
    page = ck_t.shape[2]
    assert page == LANES
    n_pages = page_table.shape[1]
    ngroups = n_pages // npg
    tpad = ks.shape[1]

    per_b = lambda b, pt: (b, 0, 0)
    any_spec = pl.BlockSpec(memory_space=pl.ANY)
    grid_spec = pltpu.PrefetchScalarGridSpec(
        num_scalar_prefetch=1,
        grid=(bs,),
        in_specs=[any_spec, any_spec, any_spec,
                  pl.BlockSpec((1, ts, att), per_b), pl.BlockSpec((1, tpad, att), per_b),
                  pl.BlockSpec((1, tpad, att), per_b), pl.BlockSpec((1, nh, LANES), per_b)],
        out_specs=pl.BlockSpec((1, ts, att), per_b),
        scratch_shapes=[pltpu.VMEM((2, npg, att, page), F32), pltpu.VMEM((2, npg, att, page), F32),
                        pltpu.VMEM((2, npg, nh, page), F32), pltpu.SemaphoreType.DMA((3, 2))])
    return pl.pallas_call(
        functools.partial(_attn_sample_kernel, npg=npg, ngroups=ngroups, nchain=nchain, ts=ts, hd=hd, nh=nh),
        out_shape=jax.ShapeDtypeStruct((bs, ts, att), F32),
        grid_spec=grid_spec,
        compiler_params=_cparams(("arbitrary",)),
        name="attn_sample",
    )(page_table, ck_t, cv_t, clf_t, q, ks, vs, lfs_t)


def _merge_ffn_kernel(x_ref, rgn_ref, attn_ref, g1_ref, sh_ref, sc_ref, g2_ref, na_ref, nf_ref,
                      wo_ref, wg_ref, wu_ref, wd_ref, o_ref):
    attn_n = (_rms_rows(attn_ref[...]) * na_ref[...]).astype(BF16)
    mix_in = jnp.concatenate([rgn_ref[...], attn_n], axis=1)
    mix = jnp.dot(mix_in, wo_ref[...], preferred_element_type=F32)
    x1 = x_ref[...] + (1.0 + g1_ref[0]) * mix
    xn = _rms_rows(x1) * nf_ref[...]
    xn = (xn * (1.0 + sc_ref[0]) + sh_ref[0]).astype(BF16)
    gate = jnp.dot(xn, wg_ref[...], preferred_element_type=F32)
    up = jnp.dot(xn, wu_ref[...], preferred_element_type=F32)
    hid = (gate * jax.nn.sigmoid(gate) * up).astype(BF16)
    f = jnp.dot(hid, wd_ref[...], preferred_element_type=F32)
    o_ref[...] = x1 + (1.0 + g2_ref[0]) * f


def _merge_ffn(x2d, rgn, attn, mods4, na, nf, wo, wg, wu, wd, *, tm, rows_per_group, mod_rows):
    n, d = x2d.shape
    rg = rgn.shape[1]
    att = attn.shape[1]
    steps_per_group = rows_per_group // tm
    row = lambda i: (i, 0)
    const = lambda i: (0, 0)
    if mod_rows == 1:
        mod_spec = pl.BlockSpec((1, 1, d), lambda i: (i // steps_per_group, 0, 0))
    else:
        mod_spec = pl.BlockSpec((1, tm, d), lambda i: (0, i, 0))
    wspec = lambda w: pl.BlockSpec(w.shape, const, pipeline_mode=pl.Buffered(1))
    return pl.pallas_call(
        _merge_ffn_kernel,
        out_shape=jax.ShapeDtypeStruct((n, d), F32),
        grid=(n // tm,),
        in_specs=[pl.BlockSpec((tm, d), row), pl.BlockSpec((tm, rg), row), pl.BlockSpec((tm, att), row),
                  mod_spec, mod_spec, mod_spec, mod_spec,
                  pl.BlockSpec((1, att), const), pl.BlockSpec((1, d), const),
                  wspec(wo), wspec(wg), wspec(wu), wspec(wd)],
        out_specs=pl.BlockSpec((tm, d), row),
        compiler_params=_cparams(("arbitrary",)),
        name="merge_ffn",
    )(x2d, rgn, attn, *mods4, na, nf, wo, wg, wu, wd)


def _pair_blockdiag(wa, wx):
    nb, c, _ = wa.shape
    z = jnp.zeros((nb // 2, c, c), wa.dtype)

    def pair(w):
        w = w.reshape(nb // 2, 2, c, c)
        top = jnp.concatenate([w[:, 0], z], axis=2)
        bot = jnp.concatenate([z, w[:, 1]], axis=2)
        return jnp.concatenate([top, bot], axis=1)

    return jnp.concatenate([pair(wa), pair(wx)], axis=2).astype(BF16)


def kernel(x_prompt, x_sample, cache_k, cache_v, cache_logf, state_conv, state_h, page_table,
           c_prompt, c_sample, norm_mix, w_ada, b_ada, w_in, b_f, g_q, g_k, w_conv, b_conv,
           w_a, b_a, w_x, b_x, lam, norm_rg, norm_attn, w_out, norm_ffn,
           w_ffn_gate, w_ffn_up, w_ffn_down):
    depth = w_in.shape[0]
    assert depth == 1, "single-layer step"
    b, t, d = x_prompt.shape
    bs, ts, _ = x_sample.shape
    nh, hd = cache_k.shape[3], cache_k.shape[4]
    att = nh * hd
    rg = state_h.shape[-1]
    n_pool, page = cache_k.shape[1], cache_k.shape[2]
    in_cols = w_in.shape[2]
    assert in_cols == 2 * rg + 3 * att + nh

    tm_in = 512
    tm_rg = 512
    tq = 512
    tk = 512
    tm_ffn = 512
    nchain = 2
    npg = min(32, page_table.shape[1] // 2)
    assert page_table.shape[1] % (2 * npg) == 0 and npg % nchain == 0

    wcols = -(-in_cols // LANES) * LANES
    w_in_b = jnp.pad(w_in[0], ((0, 0), (0, wcols - in_cols))).astype(BF16)
    ones_heads = jnp.kron(jnp.eye(nh, dtype=F32), jnp.ones((hd, hd), F32)).astype(BF16)
    gq_row = jnp.tile(g_q[0], nh)[None, :] * (hd ** -0.5)
    gq_row2 = gq_row * LOG2E
    gk_row = jnp.tile(g_k[0], nh)[None, :]
    assert 2 * hd == LANES and 3 * nh <= LANES
    sel_np = [[0.0] * att for _ in range(LANES)]
    for h in range(nh):
        for kk in range(3):
            sel_np[kk * nh + h][LANES * (h // 2) + 3 * (h % 2) + kk] = 1.0
    sel_terms = jnp.array(sel_np, BF16)
    wa_pairs = _pair_blockdiag(w_a[0], w_x[0])
    wo_b = w_out[0].astype(BF16)
    wg_b = w_ffn_gate[0].astype(BF16)
    wu_b = w_ffn_up[0].astype(BF16)
    wd_b = w_ffn_down[0].astype(BF16)
    row = lambda a: a[0][None, :]

    mods = _ada(jnp.concatenate([c_prompt, c_sample], axis=0), w_ada[0], b_ada[0][None, :])
    mods = mods.reshape(b + bs, 6, d)
    mods_p = [mods[:b, i:i + 1] for i in range(6)]
    mods_s = [jnp.repeat(mods[b:, i], ts, axis=0)[None] for i in range(6)]

    xp2d = x_prompt.reshape(b * t, d)
    (xr, gr, knT, ka, vT32, qT, vT, logf) = _inproj_prompt(
        xp2d, mods_p[0], mods_p[1], row(norm_mix), w_in_b, ones_heads, gq_row2, gk_row, row(b_f), sel_terms,
        b=b, t=t, tm=tm_in, rg=rg, att=att, hd=hd, nh=nh)
    rgn, h_last = _rg_prompt(xr, gr, w_conv[0], row(b_conv), wa_pairs, row(b_a), row(b_x), row(lam),
                             row(norm_rg), b=b, t=t, tm=tm_rg)
    attn = _attn_prompt(qT, ka.reshape(b, t, 2 * att), vT, tq=tq, tk=tk, hd=hd, npair=2, nsplit=1, nunroll=2)
    y_prompt = _merge_ffn(xp2d, rgn, attn.reshape(b * t, att), [mods_p[2], mods_p[3], mods_p[4], mods_p[5]],
                          row(norm_attn), row(norm_ffn), wo_b, wg_b, wu_b, wd_b,
                          tm=tm_ffn, rows_per_group=t, mod_rows=1)

    ns = bs * ts
    xs2d = x_sample.reshape(ns, d)
    (xr_s, gr_s, q_s, kn_s, v_s, logf_s) = _inproj_sample(
        xs2d, mods_s[0], mods_s[1], row(norm_mix), w_in_b, ones_heads, gq_row, gk_row, row(b_f),
        rg=rg, att=att, hd=hd, nh=nh)
    x_ext = jnp.concatenate([state_conv[0], xr_s.reshape(bs, ts, rg)], axis=1)
    xwin = [x_ext[:, j:j + ts].reshape(ns, rg) for j in range(CONV_WIDTH)]
    h0_rows = jnp.repeat(state_h[0], ts, axis=0)
    rgn_s, h_rows = _rg_sample(xwin, gr_s, h0_rows, w_conv[0], row(b_conv), wa_pairs, row(b_a), row(b_x),
                               row(lam), row(norm_rg), seg=ts)
    tpad = SUBLANES
    pad_t = lambda a: jnp.pad(a, ((0, 0), (0, tpad - ts), (0, 0)))
    ck_t = jnp.transpose(cache_k[0], (0, 2, 3, 1)).reshape(n_pool, att, page)
    cv_t = jnp.transpose(cache_v[0], (0, 2, 3, 1)).reshape(n_pool, att, page)
    clf_t = jnp.transpose(cache_logf[0], (0, 2, 1))
    lfs_t = jnp.pad(jnp.transpose(logf_s.reshape(bs, ts, nh), (0, 2, 1)), ((0, 0), (0, 0), (0, LANES - ts)))
    attn_s = _attn_sample(
        page_table, ck_t, cv_t, clf_t, q_s.reshape(bs, ts, att), pad_t(kn_s.reshape(bs, ts, att)),
        pad_t(v_s.reshape(bs, ts, att)), lfs_t, npg=npg, nchain=nchain, hd=hd, nh=nh)
    y_sample = _merge_ffn(xs2d, rgn_s, attn_s.reshape(ns, att), [mods_s[2], mods_s[3], mods_s[4], mods_s[5]],
                          row(norm_attn), row(norm_ffn), wo_b, wg_b, wu_b, wd_b,
                          tm=min(tm_ffn, ns), rows_per_group=ns, mod_rows=min(tm_ffn, ns))

    nc = CONV_WIDTH - 1
    heads_last = lambda a: jnp.transpose(a.reshape(b, nh, hd, t), (0, 3, 1, 2))[None]
    return (y_prompt.reshape(b, t, d), y_sample.reshape(bs, ts, d),
            heads_last(knT), heads_last(vT32), logf.reshape(1, b, t, nh),
            xr.reshape(b, t, rg)[:, t - nc:, :][None], h_last.reshape(1, b, rg),
            kn_s.reshape(1, bs, ts, nh, hd), v_s.reshape(1, bs, ts, nh, hd), logf_s.reshape(1, bs, ts, nh),
            x_ext[:, ts:, :][None], h_rows.reshape(bs, ts, rg)[:, ts - 1, :][None])
```

```python
import functools

import jax
import jax.numpy as jnp
from jax import lax
from jax.experimental import pallas as pl
from jax.experimental.pallas import tpu as pltpu

EPS = 1e-6
RG_C = 8.0
CONV_WIDTH = 4
LANES = 128
SUBLANES = 8
NEG_BIG = -1e30
LOG2E = 1.4426950408889634
VMEM_LIMIT = 56 * 1024 * 1024

F32 = jnp.float32
BF16 = jnp.bfloat16


def _cparams(sem, vmem=VMEM_LIMIT):
    return pltpu.CompilerParams(dimension_semantics=sem, vmem_limit_bytes=vmem)


def _rms_rows(x):
    return x * lax.rsqrt(jnp.mean(x * x, axis=-1, keepdims=True) + EPS)


def _log_sigmoid(x):
    return -(jnp.maximum(-x, 0.0) + jnp.log1p(jnp.exp(-jnp.abs(x))))


def _cumsum_rows(x, seg=None):
    n = x.shape[0]
    seg = n if seg is None else seg
    row = lax.broadcasted_iota(jnp.int32, x.shape, 0)
    pos = row if seg == n else jnp.bitwise_and(row, seg - 1)
    d = 1
    while d < seg:
        x = x + jnp.where(pos >= d, pltpu.roll(x, d, axis=0), 0.0)
        d *= 2
    return x


def _scan_rows(a, u, seg=None):
    n = a.shape[0]
    seg = n if seg is None else seg
    row = lax.broadcasted_iota(jnp.int32, a.shape, 0)
    pos = row if seg == n else jnp.bitwise_and(row, seg - 1)
    d = 1
    while d < seg:
        m = pos >= d
        a_s = jnp.where(m, pltpu.roll(a, d, axis=0), 1.0)
        u_s = jnp.where(m, pltpu.roll(u, d, axis=0), 0.0)
        u = u + a * u_s
        a = a * a_s
        d *= 2
    return a, u


def _ada_kernel(c_ref, w_ref, b_ref, o_ref):
    c = c_ref[...]
    s = (c * jax.nn.sigmoid(c)).astype(BF16)
    o_ref[...] = jnp.dot(s, w_ref[...].astype(BF16), preferred_element_type=F32) + b_ref[...]


def _ada(c, w, b):
    m, d = c.shape
    n = w.shape[1]
    tn = n // 4
    return pl.pallas_call(
        _ada_kernel,
        out_shape=jax.ShapeDtypeStruct((m, n), F32),
        grid=(n // tn,),
        in_specs=[pl.BlockSpec((m, d), lambda j: (0, 0)),
                  pl.BlockSpec((d, tn), lambda j: (0, j)),
                  pl.BlockSpec((1, tn), lambda j: (0, j))],
        out_specs=pl.BlockSpec((m, tn), lambda j: (0, j)),
        compiler_params=_cparams(("arbitrary",)),
        name="ada_mod",
    )(c, w, b)


def _head_rms(t, ones_bf16, gain_row, hd):
    ssq = jnp.dot((t * t).astype(BF16), ones_bf16, preferred_element_type=F32)
    return t * lax.rsqrt(ssq * (1.0 / hd) + EPS) * gain_row


def _inproj_body(x_ref, shift_ref, scale_ref, g_ref, w_ref, ones_ref, gq_ref, gk_ref, bf_ref, rg, att, hd, nh):
    x = x_ref[...]
    xn = _rms_rows(x) * g_ref[...]
    xn = xn * (1.0 + scale_ref[0]) + shift_ref[0]
    z = jnp.dot(xn.astype(BF16), w_ref[...], preferred_element_type=F32)
    xr = z[:, 0:rg]
    gr = z[:, rg:2 * rg]
    q = z[:, 2 * rg:2 * rg + att]
    k = z[:, 2 * rg + att:2 * rg + 2 * att]
    v = z[:, 2 * rg + 2 * att:2 * rg + 3 * att]
    fl = z[:, 2 * rg + 3 * att:2 * rg + 3 * att + nh]
    ones = ones_ref[...]
    qn = _head_rms(q, ones, gq_ref[...], hd)
    kn = _head_rms(k, ones, gk_ref[...], hd)
    logf = _log_sigmoid(fl + bf_ref[...])
    return xr, gr, qn, kn, v, logf


def _inproj_prompt_kernel(x_ref, shift_ref, scale_ref, g_ref, w_ref, ones_ref, gq_ref, gk_ref, bf_ref, sel_ref,
                          xr_ref, gr_ref, knT_ref, ka_ref, vT32_ref, qT_ref, vT_ref, logf_ref,
                          carry_ref, *, rg, att, hd, nh):
    t = pl.program_id(1)

    @pl.when(t == 0)
    def _():
        carry_ref[...] = jnp.zeros_like(carry_ref)

    xr, gr, qn, kn, v, logf = _inproj_body(x_ref, shift_ref, scale_ref, g_ref, w_ref, ones_ref,
                                           gq_ref, gk_ref, bf_ref, rg, att, hd, nh)
    xr_ref[...] = xr
    gr_ref[...] = gr
    knT_ref[0] = kn.T
    v_t = v.T
    vT32_ref[0] = v_t
    qT_ref[0] = qn.T.astype(BF16)
    vT_ref[0] = v_t.astype(BF16)
    logf_ref[...] = logf
    fc = _cumsum_rows(logf) + carry_ref[0:1, 0:nh]
    n = fc.shape[0]
    carry_ref[0:1, 0:nh] = fc[n - 1:n, :]
    fb = fc * (-LOG2E)
    hi = fb.astype(BF16).astype(F32)
    r1 = fb - hi
    mid = r1.astype(BF16).astype(F32)
    lo = r1 - mid
    terms = jnp.concatenate([hi, mid, lo, jnp.zeros((n, LANES - 3 * nh), F32)], axis=1).astype(BF16)
    aug = jnp.dot(terms, sel_ref[...], preferred_element_type=F32).astype(BF16)
    kb = kn.astype(BF16)
    parts = []
    for hp in range(att // LANES):
        parts += [kb[:, hp * LANES:(hp + 1) * LANES], aug[:, hp * LANES:(hp + 1) * LANES]]
    ka_ref[...] = jnp.concatenate(parts, axis=1)


def _inproj_sample_kernel(x_ref, shift_ref, scale_ref, g_ref, w_ref, ones_ref, gq_ref, gk_ref, bf_ref,
                          xr_ref, gr_ref, q_ref, kn_ref, v_ref, logf_ref, *, rg, att, hd, nh):
    xr, gr, qn, kn, v, logf = _inproj_body(x_ref, shift_ref, scale_ref, g_ref, w_ref, ones_ref,
                                           gq_ref, gk_ref, bf_ref, rg, att, hd, nh)
    xr_ref[...] = xr
    gr_ref[...] = gr
    q_ref[...] = qn
    kn_ref[...] = kn
    v_ref[...] = v
    logf_ref[...] = logf


def _inproj_common_specs(d, wcols, att, nh, mod_rows, mod_map):
    const = lambda *_: (0, 0)
    return [pl.BlockSpec((1, mod_rows, d), mod_map),
            pl.BlockSpec((1, mod_rows, d), mod_map),
            pl.BlockSpec((1, d), const),
            pl.BlockSpec((d, wcols), const),
            pl.BlockSpec((att, att), const),
            pl.BlockSpec((1, att), const),
            pl.BlockSpec((1, att), const),
            pl.BlockSpec((1, nh), const)]


def _inproj_prompt(x2d, shift, scale, g, w, ones, gq, gk, bf, sel, *, b, t, tm, rg, att, hd, nh):
    n, d = x2d.shape
    nt = t // tm
    row = lambda bi, ti: (bi * nt + ti, 0)
    tr = lambda bi, ti: (bi, 0, ti)
    kern = functools.partial(_inproj_prompt_kernel, rg=rg, att=att, hd=hd, nh=nh)
    return pl.pallas_call(
        kern,
        out_shape=(jax.ShapeDtypeStruct((n, rg), F32), jax.ShapeDtypeStruct((n, rg), F32),
                   jax.ShapeDtypeStruct((b, att, t), F32), jax.ShapeDtypeStruct((n, 2 * att), BF16),
                   jax.ShapeDtypeStruct((b, att, t), F32),
                   jax.ShapeDtypeStruct((b, att, t), BF16), jax.ShapeDtypeStruct((b, att, t), BF16),
                   jax.ShapeDtypeStruct((n, nh), F32)),
        grid=(b, nt),
        in_specs=[pl.BlockSpec((tm, d), row)] + _inproj_common_specs(
            d, w.shape[1], att, nh, 1, lambda bi, ti: (bi, 0, 0)) + [pl.BlockSpec(sel.shape, lambda bi, ti: (0, 0))],
        out_specs=(pl.BlockSpec((tm, rg), row), pl.BlockSpec((tm, rg), row),
                   pl.BlockSpec((1, att, tm), tr), pl.BlockSpec((tm, 2 * att), row),
                   pl.BlockSpec((1, att, tm), tr),
                   pl.BlockSpec((1, att, tm), tr), pl.BlockSpec((1, att, tm), tr),
                   pl.BlockSpec((tm, nh), row)),
        scratch_shapes=[pltpu.VMEM((SUBLANES, LANES), F32)],
        compiler_params=_cparams(("arbitrary", "arbitrary")),
        name="inproj_prompt",
    )(x2d, shift, scale, g, w, ones, gq, gk, bf, sel)


def _inproj_sample(x2d, shift, scale, g, w, ones, gq, gk, bf, *, rg, att, hd, nh):
    n, d = x2d.shape
    row = lambda i: (0, 0)
    kern = functools.partial(_inproj_sample_kernel, rg=rg, att=att, hd=hd, nh=nh)
    return pl.pallas_call(
        kern,
        out_shape=(jax.ShapeDtypeStruct((n, rg), F32), jax.ShapeDtypeStruct((n, rg), F32),
                   jax.ShapeDtypeStruct((n, att), F32), jax.ShapeDtypeStruct((n, att), F32),
                   jax.ShapeDtypeStruct((n, att), F32), jax.ShapeDtypeStruct((n, nh), F32)),
        grid=(1,),
        in_specs=[pl.BlockSpec((n, d), row)] + _inproj_common_specs(
            d, w.shape[1], att, nh, n, lambda i: (0, 0, 0)),
        out_specs=(pl.BlockSpec((n, rg), row), pl.BlockSpec((n, rg), row),
                   pl.BlockSpec((n, att), row), pl.BlockSpec((n, att), row),
                   pl.BlockSpec((n, att), row), pl.BlockSpec((n, nh), row)),
        compiler_params=_cparams(("arbitrary",)),
        name="inproj_sample",
    )(x2d, shift, scale, g, w, ones, gq, gk, bf)


def _rg_gates(xc, wa_ref, ba, bx, lam):
    sp = jnp.maximum(-lam, 0.0) + jnp.log1p(jnp.exp(-jnp.abs(lam)))
    a_parts, u_parts = [], []
    npair = wa_ref.shape[0]
    pw = wa_ref.shape[1]
    for j in range(npair):
        sl = slice(pw * j, pw * (j + 1))
        xcj = xc[:, sl]
        g = jnp.dot(xcj.astype(BF16), wa_ref[j], preferred_element_type=F32)
        r = jax.nn.sigmoid(g[:, :pw] + ba[:, sl])
        i = jax.nn.sigmoid(g[:, pw:] + bx[:, sl])
        log_a = (-RG_C) * r * sp[:, sl]
        a = jnp.exp(log_a)
        a_parts.append(a)
        w2 = -jnp.tanh(log_a) * (a * a + 1.0)
        w = jnp.where(w2 > 0.0, w2 * lax.rsqrt(w2), 0.0)
        u_parts.append(w * (i * xcj))
    return a_parts, u_parts


def _rg_finish(h, gr, norm_row):
    out = h * jax.nn.gelu(gr)
    return (_rms_rows(out) * norm_row).astype(BF16)


def _rg_prompt_kernel(xr_ref, gr_ref, wc_ref, bc_ref, wa_ref, ba_ref, bx_ref, lam_ref, nrm_ref,
                      o_ref, hT_ref, xbuf_ref, h_ref, sa_ref, su_ref, *, tm):
    t = pl.program_id(1)

    @pl.when(t == 0)
    def _():
        xbuf_ref[0:SUBLANES, :] = jnp.zeros((SUBLANES, xbuf_ref.shape[1]), F32)
        h_ref[...] = jnp.zeros_like(h_ref)

    xbuf_ref[SUBLANES:SUBLANES + tm, :] = xr_ref[...]
    xfull = xbuf_ref[...]
    xc = bc_ref[...] + xfull[SUBLANES:, :] * wc_ref[CONV_WIDTH - 1:CONV_WIDTH, :]
    for dly in range(1, CONV_WIDTH):
        tap = CONV_WIDTH - 1 - dly
        xc = xc + pltpu.roll(xfull, dly, axis=0)[SUBLANES:, :] * wc_ref[tap:tap + 1, :]
    xbuf_ref[0:SUBLANES, :] = xbuf_ref[tm:tm + SUBLANES, :]

    a_parts, u_parts = _rg_gates(xc, wa_ref, ba_ref[...], bx_ref[...], lam_ref[...])
    pw = wa_ref.shape[1]
    for j, (a, u) in enumerate(zip(a_parts, u_parts)):
        big_a, big_u = _scan_rows(a, u, seg=SUBLANES)
        sa_ref[:, pw * j:pw * (j + 1)] = big_a
        su_ref[:, pw * j:pw * (j + 1)] = big_u

    def group(g, h_prev):
        r0 = pl.multiple_of(g * SUBLANES, SUBLANES)
        hg = sa_ref[pl.ds(r0, SUBLANES), :] * h_prev + su_ref[pl.ds(r0, SUBLANES), :]
        su_ref[pl.ds(r0, SUBLANES), :] = hg
        return hg[SUBLANES - 1:SUBLANES, :]

    h_last = lax.fori_loop(0, tm // SUBLANES, group, h_ref[0:1, :], unroll=8)
    h = su_ref[...]
    h_ref[0:1, :] = h_last
    hT_ref[0] = h_last
    o_ref[...] = _rg_finish(h, gr_ref[...], nrm_ref[...])


def _rg_prompt(xr, gr, wc, bc, wa, ba, bx, lam, nrm, *, b, t, tm):
    n, rg = xr.shape
    nt = t // tm
    row = lambda bi, ti: (bi * nt + ti, 0)
    const2 = lambda bi, ti: (0, 0)
    return pl.pallas_call(
        functools.partial(_rg_prompt_kernel, tm=tm),
        out_shape=(jax.ShapeDtypeStruct((n, rg), BF16), jax.ShapeDtypeStruct((b, 1, rg), F32)),
        grid=(b, nt),
        in_specs=[pl.BlockSpec((tm, rg), row), pl.BlockSpec((tm, rg), row),
                  pl.BlockSpec(wc.shape, const2), pl.BlockSpec((1, rg), const2),
                  pl.BlockSpec(wa.shape, lambda bi, ti: (0, 0, 0)),
                  pl.BlockSpec((1, rg), const2), pl.BlockSpec((1, rg), const2),
                  pl.BlockSpec((1, rg), const2), pl.BlockSpec((1, rg), const2)],
        out_specs=(pl.BlockSpec((tm, rg), row), pl.BlockSpec((1, 1, rg), lambda bi, ti: (bi, 0, 0))),
        scratch_shapes=[pltpu.VMEM((tm + SUBLANES, rg), F32), pltpu.VMEM((SUBLANES, rg), F32),
                        pltpu.VMEM((tm, rg), F32), pltpu.VMEM((tm, rg), F32)],
        compiler_params=_cparams(("arbitrary", "arbitrary")),
        name="rg_prompt",
    )(xr, gr, wc, bc, wa, ba, bx, lam, nrm)


def _rg_sample_kernel(x0_ref, x1_ref, x2_ref, x3_ref, gr_ref, h0_ref, wc_ref, bc_ref, wa_ref, ba_ref, bx_ref,
                      lam_ref, nrm_ref, o_ref, h_ref, *, seg):
    xs = (x0_ref, x1_ref, x2_ref, x3_ref)
    xc = bc_ref[...]
    for j in range(CONV_WIDTH):
        xc = xc + xs[j][...] * wc_ref[j:j + 1, :]
    a_parts, u_parts = _rg_gates(xc, wa_ref, ba_ref[...], bx_ref[...], lam_ref[...])
    pw = wa_ref.shape[1]
    h_parts = []
    for j, (a, u) in enumerate(zip(a_parts, u_parts)):
        big_a, big_u = _scan_rows(a, u, seg=seg)
        h_parts.append(big_a * h0_ref[:, pw * j:pw * (j + 1)] + big_u)
    h = jnp.concatenate(h_parts, axis=1)
    h_ref[...] = h
    o_ref[...] = _rg_finish(h, gr_ref[...], nrm_ref[...])


def _rg_sample(xwin, gr, h0_rows, wc, bc, wa, ba, bx, lam, nrm, *, seg):
    n, rg = gr.shape
    full = pl.BlockSpec((n, rg), lambda i: (0, 0))
    vec = pl.BlockSpec((1, rg), lambda i: (0, 0))
    return pl.pallas_call(
        functools.partial(_rg_sample_kernel, seg=seg),
        out_shape=(jax.ShapeDtypeStruct((n, rg), BF16), jax.ShapeDtypeStruct((n, rg), F32)),
        grid=(1,),
        in_specs=[full, full, full, full, full, full,
                  pl.BlockSpec(wc.shape, lambda i: (0, 0)), vec,
                  pl.BlockSpec(wa.shape, lambda i: (0, 0, 0)), vec, vec, vec, vec],
        out_specs=(full, full),
        compiler_params=_cparams(("arbitrary",)),
        name="rg_sample",
    )(*xwin, gr, h0_rows, wc, bc, wa, ba, bx, lam, nrm)


def _attn_prompt_kernel(qT_ref, ka_ref, vT_ref, o_ref, *, tq, tk, hd, npair, nsplit, nunroll):
    i = pl.program_id(2)
    pair = 2 * hd
    nhead = 2 * npair
    nsum = 16

    qs = []
    for pp in range(npair):
        qp = qT_ref[0, pp * pair:(pp + 1) * pair, :]
        rowq = lax.broadcasted_iota(jnp.int32, qp.shape, 0)
        zero = jnp.zeros_like(qp)
        for hh in range(2):
            qh = jnp.where(rowq < hd, qp, zero) if hh == 0 else jnp.where(rowq >= hd, qp, zero)
            pick = (jnp.where(rowq < 3 * hh + 3, 1.0, 0.0) - jnp.where(rowq < 3 * hh, 1.0, 0.0)).astype(BF16)
            qs.append(jnp.concatenate([qh, pick], axis=0))
    ones_rows = jnp.ones((nsum, tk), BF16)

    wq = tq // nsplit
    items = [(h, c) for h in range(nhead) for c in range(nsplit)]

    def scores(j, it):
        h, c = it
        pp = h // 2
        start = pl.multiple_of(j * tk, tk)
        kblk = ka_ref[0, pl.ds(start, tk), 2 * pair * pp:2 * pair * (pp + 1)]
        return jnp.dot(kblk, qs[h][:, c * wq:(c + 1) * wq], preferred_element_type=F32)

    def absorb(j, it, s, state, masked):
        h, c = it
        m, acc = state
        start = pl.multiple_of(j * tk, tk)
        if masked:
            rk = lax.broadcasted_iota(jnp.int32, s.shape, 0)
            cq = lax.broadcasted_iota(jnp.int32, s.shape, 1) + c * wq
            s = jnp.where(rk <= cq, s, NEG_BIG)
        m_new = jnp.maximum(m, jnp.max(s, axis=0, keepdims=True))
        p = jnp.exp2(s - m_new).astype(BF16)
        alpha = jnp.exp2(m - m_new)
        vblk = jnp.concatenate([vT_ref[0, h * hd:(h + 1) * hd, pl.ds(start, tk)], ones_rows], axis=0)
        pv = jnp.dot(vblk, p, preferred_element_type=F32)
        return m_new, alpha * acc + pv

    def make_body(nblk):
        def body(j, c):
            s_cur, states = c
            states = list(states)
            for u in range(nblk):
                blk = j * nblk + u
                for n, it in enumerate(items):
                    s_next = scores(blk, items[n + 1]) if n + 1 < len(items) else scores(blk + 1, items[0])
                    states[n] = absorb(blk, it, s_cur, states[n], False)
                    s_cur = s_next
            return s_cur, tuple(states)
        return body

    assert tq == tk
    init = tuple((jnp.full((1, wq), NEG_BIG, F32), jnp.zeros((hd + nsum, wq), F32)) for _ in items)
    carry = lax.fori_loop(0, i // nunroll, make_body(nunroll), (scores(0, items[0]), init))
    s_cur, states = lax.fori_loop((i // nunroll) * nunroll, i, make_body(1), carry)
    outs = []
    for n, it in enumerate(items):
        s_next = scores(i, items[n + 1]) if n + 1 < len(items) else None
        _, acc = absorb(i, it, s_cur, states[n], True)
        outs.append(acc[0:hd] / acc[hd:hd + 1])
        s_cur = s_next
    rows = [jnp.concatenate(outs[h * nsplit:(h + 1) * nsplit], axis=1) for h in range(nhead)]
    o_ref[0] = jnp.concatenate(rows, axis=0).T


def _attn_prompt(qT, ka, vT, *, tq, tk, hd, npair, nsplit, nunroll):
    b, att, t = qT.shape
    width = 2 * hd * npair
    return pl.pallas_call(
        functools.partial(_attn_prompt_kernel, tq=tq, tk=tk, hd=hd, npair=npair, nsplit=nsplit,
                          nunroll=nunroll),
        out_shape=jax.ShapeDtypeStruct((b, t, att), F32),
        grid=(b, att // width, t // tq),
        in_specs=[pl.BlockSpec((1, width, tq), lambda bi, hg, i: (bi, hg, i)),
                  pl.BlockSpec((1, t, 2 * width), lambda bi, hg, i: (bi, 0, hg)),
                  pl.BlockSpec((1, width, t), lambda bi, hg, i: (bi, hg, 0))],
        out_specs=pl.BlockSpec((1, tq, width), lambda bi, hg, i: (bi, i, hg)),
        compiler_params=_cparams(("arbitrary", "arbitrary", "arbitrary")),
        name="attn_prompt",
    )(qT, ka, vT)


def _cumsum_lanes(x):
    n = x.shape[1]
    lane = lax.broadcasted_iota(jnp.int32, x.shape, 1)
    d = 1
    while d < n:
        x = x + jnp.where(lane >= d, pltpu.roll(x, d, axis=1), 0.0)
        d *= 2
    return x


def _rows_as_lanes(a, height):
    r = a.shape[0]
    sq = jnp.concatenate([a, jnp.ones((LANES - r, LANES), F32)], axis=0).T
    return jnp.concatenate([sq] * (height // LANES), axis=0)


def _attn_sample_kernel(pt_ref, ck_hbm, cv_hbm, clf_hbm, q_ref, ks_ref, vs_ref, lfs_ref, o_ref,
                        kbuf, vbuf, lfbuf, sem, *, npg, ngroups, nchain, ts, hd, nh):
    b = pl.program_id(0)
    nb = pl.num_programs(0)
    att = nh * hd
    nrow = ts * nh
    page = kbuf.shape[3]
    assert ngroups % 2 == 0

    def group_copies(bb, g):
        slot = g % 2
        cps = []
        for ii in range(npg):
            pg = pt_ref[bb, g * npg + ii]
            cps.append(pltpu.make_async_copy(ck_hbm.at[pg], kbuf.at[slot, ii], sem.at[0, slot]))
            cps.append(pltpu.make_async_copy(cv_hbm.at[pg], vbuf.at[slot, ii], sem.at[1, slot]))
            cps.append(pltpu.make_async_copy(clf_hbm.at[pg], lfbuf.at[slot, ii], sem.at[2, slot]))
        return cps

    @pl.when(b == 0)
    def _():
        for cp in group_copies(b, 0):
            cp.start()

    q = q_ref[0]
    col_head = lax.broadcasted_iota(jnp.int32, (nh, att), 1) // hd
    row_head = lax.broadcasted_iota(jnp.int32, (nh, att), 0)
    blocks = [jnp.where(col_head == row_head, jnp.broadcast_to(q[tt:tt + 1, :], (nh, att)), 0.0)
              for tt in range(ts)]
    qp = jnp.concatenate(blocks, axis=0).astype(BF16)

    def update(state, k_t, v_t, bias, valid):
        m_old, l_old, acc = state
        n = k_t.shape[1]
        s = jnp.dot(qp, k_t, preferred_element_type=F32) + bias
        if valid is not None:
            s = jnp.where(valid, s, NEG_BIG)
        m_new = jnp.maximum(m_old, jnp.max(s, axis=1, keepdims=True))
        p = jnp.exp(s - jnp.concatenate([m_new] * (n // LANES), axis=1))
        alpha = jnp.exp(m_old - m_new)
        l_new = alpha * l_old + jnp.sum(p, axis=1, keepdims=True)
        p_pad = jnp.concatenate([p.astype(BF16), jnp.zeros((LANES - nrow, n), BF16)], axis=0)
        pv = lax.dot_general(v_t, p_pad, (((1,), (1,)), ((), ())), preferred_element_type=F32)
        return m_new, l_new, acc * _rows_as_lanes(alpha, att) + pv

    states = [(jnp.full((nrow, LANES), NEG_BIG, F32), jnp.zeros((nrow, LANES), F32), jnp.zeros((att, LANES), F32))
              for _ in range(nchain)]
    per_chain = npg // nchain
    off = jnp.zeros((nh, LANES), F32)
    for g in range(ngroups):
        slot = g % 2
        if g + 1 < ngroups:
            for cp in group_copies(b, g + 1):
                cp.start()
        else:
            @pl.when(b + 1 < nb)
            def _():
                for cp in group_copies(b + 1, 0):
                    cp.start()
        for cp in group_copies(b, g):
            cp.wait()
        lf = jnp.concatenate([lfbuf[slot, ii] for ii in range(npg)], axis=0)
        cum = _cumsum_lanes(lf)
        pieces = []
        for pi in range(npg):
            cp_ = cum[pi * nh:(pi + 1) * nh, :]
            pieces.append(jnp.concatenate([-(cp_ + off)] * ts, axis=0))
            off = off + jnp.broadcast_to(cp_[:, page - 1:page], (nh, LANES))
        for c in range(nchain):
            ids = range(c * per_chain, (c + 1) * per_chain)
            k_t = jnp.concatenate([kbuf[slot, ii] for ii in ids], axis=1).astype(BF16)
            v_t = jnp.concatenate([vbuf[slot, ii] for ii in ids], axis=1).astype(BF16)
            states[c] = update(states[c], k_t, v_t, jnp.concatenate([pieces[ii] for ii in ids], axis=1), None)

    zpad = jnp.zeros((LANES - ks_ref.shape[1], att), F32)
    ks_t = jnp.concatenate([ks_ref[0], zpad], axis=0).T.astype(BF16)
    vs_t = jnp.concatenate([vs_ref[0], zpad], axis=0).T.astype(BF16)
    fnew = _cumsum_lanes(lfs_ref[0]) + off
    bias = jnp.concatenate([-fnew] * ts, axis=0)
    key_t = lax.broadcasted_iota(jnp.int32, (nrow, LANES), 1)
    qry_t = lax.broadcasted_iota(jnp.int32, (nrow, LANES), 0) // nh
    valid = (key_t <= qry_t) & (key_t < ts)
    states[0] = update(states[0], ks_t, vs_t, bias, valid)
    m_all = states[0][0]
    for c in range(1, nchain):
        m_all = jnp.maximum(m_all, states[c][0])
    l_fin = jnp.zeros((nrow, LANES), F32)
    acc = jnp.zeros((att, LANES), F32)
    for c in range(nchain):
        wgt = jnp.exp(states[c][0] - m_all)
        l_fin = l_fin + wgt * states[c][1]
        acc = acc + states[c][2] * _rows_as_lanes(wgt, att)
    o_all = acc / _rows_as_lanes(l_fin, att)
    r_head = lax.broadcasted_iota(jnp.int32, (att, LANES), 0) // hd
    c_head = jnp.bitwise_and(lax.broadcasted_iota(jnp.int32, (att, LANES), 1), nh - 1)
    o_t = jnp.where(r_head == c_head, o_all, 0.0).T[0:nrow, :]
    o_ref[0] = jnp.sum(o_t.reshape(ts, nh, att), axis=1)


def _attn_sample(page_table, ck_t, cv_t, clf_t, q, ks, vs, lfs_t, *, npg, nchain, hd, nh):
    bs, ts, att = q.shape
    page = ck_t.shape[2]
    assert page == LANES
    n_pages = page_table.shape[1]
    ngroups = n_pages // npg
    tpad = ks.shape[1]

    per_b = lambda b, pt: (b, 0, 0)
    any_spec = pl.BlockSpec(memory_space=pl.ANY)
    grid_spec = pltpu.PrefetchScalarGridSpec(
        num_scalar_prefetch=1,
        grid=(bs,),
        in_specs=[any_spec, any_spec, any_spec,
                  pl.BlockSpec((1, ts, att), per_b), pl.BlockSpec((1, tpad, att), per_b),
                  pl.BlockSpec((1, tpad, att), per_b), pl.BlockSpec((1, nh, LANES), per_b)],
        out_specs=pl.BlockSpec((1, ts, att), per_b),
        scratch_shapes=[pltpu.VMEM((2, npg, att, page), F32), pltpu.VMEM((2, npg, att, page), F32),
                        pltpu.VMEM((2, npg, nh, page), F32), pltpu.SemaphoreType.DMA((3, 2))])
    return pl.pallas_call(
        functools.partial(_attn_sample_kernel, npg=npg, ngroups=ngroups, nchain=nchain, ts=ts, hd=hd, nh=nh),
        out_shape=jax.ShapeDtypeStruct((bs, ts, att), F32),
        grid_spec=grid_spec,
        compiler_params=_cparams(("arbitrary",)),
        name="attn_sample",
    )(page_table, ck_t, cv_t, clf_t, q, ks, vs, lfs_t)


def _merge_ffn_kernel(x_ref, rgn_ref, attn_ref, g1_ref, sh_ref, sc_ref, g2_ref, na_ref, nf_ref,
                      wo_ref, wg_ref, wu_ref, wd_ref, o_ref):
    attn_n = (_rms_rows(attn_ref[...]) * na_ref[...]).astype(BF16)
    mix_in = jnp.concatenate([rgn_ref[...], attn_n], axis=1)
    mix = jnp.dot(mix_in, wo_ref[...], preferred_element_type=F32)
    x1 = x_ref[...] + (1.0 + g1_ref[0]) * mix
    xn = _rms_rows(x1) * nf_ref[...]
    xn = (xn * (1.0 + sc_ref[0]) + sh_ref[0]).astype(BF16)
    gate = jnp.dot(xn, wg_ref[...], preferred_element_type=F32)
    up = jnp.dot(xn, wu_ref[...], preferred_element_type=F32)
    hid = (gate * jax.nn.sigmoid(gate) * up).astype(BF16)
    f = jnp.dot(hid, wd_ref[...], preferred_element_type=F32)
    o_ref[...] = x1 + (1.0 + g2_ref[0]) * f


def _merge_ffn(x2d, rgn, attn, mods4, na, nf, wo, wg, wu, wd, *, tm, rows_per_group, mod_rows):
    n, d = x2d.shape
    rg = rgn.shape[1]
    att = attn.shape[1]
    steps_per_group = rows_per_group // tm
    row = lambda i: (i, 0)
    const = lambda i: (0, 0)
    if mod_rows == 1:
        mod_spec = pl.BlockSpec((1, 1, d), lambda i: (i // steps_per_group, 0, 0))
    else:
        mod_spec = pl.BlockSpec((1, tm, d), lambda i: (0, i, 0))
    wspec = lambda w: pl.BlockSpec(w.shape, const, pipeline_mode=pl.Buffered(1))
    return pl.pallas_call(
        _merge_ffn_kernel,
        out_shape=jax.ShapeDtypeStruct((n, d), F32),
        grid=(n // tm,),
        in_specs=[pl.BlockSpec((tm, d), row), pl.BlockSpec((tm, rg), row), pl.BlockSpec((tm, att), row),
                  mod_spec, mod_spec, mod_spec, mod_spec,
                  pl.BlockSpec((1, att), const), pl.BlockSpec((1, d), const),
                  wspec(wo), wspec(wg), wspec(wu), wspec(wd)],
        out_specs=pl.BlockSpec((tm, d), row),
        compiler_params=_cparams(("arbitrary",)),
        name="merge_ffn",
    )(x2d, rgn, attn, *mods4, na, nf, wo, wg, wu, wd)


def _pair_blockdiag(wa, wx):
    nb, c, _ = wa.shape
    z = jnp.zeros((nb // 2, c, c), wa.dtype)

    def pair(w):
        w = w.reshape(nb // 2, 2, c, c)
        top = jnp.concatenate([w[:, 0], z], axis=2)
        bot = jnp.concatenate([z, w[:, 1]], axis=2)
        return jnp.concatenate([top, bot], axis=1)

    return jnp.concatenate([pair(wa), pair(wx)], axis=2).astype(BF16)


def kernel(x_prompt, x_sample, cache_k, cache_v, cache_logf, state_conv, state_h, page_table,
           c_prompt, c_sample, norm_mix, w_ada, b_ada, w_in, b_f, g_q, g_k, w_conv, b_conv,
           w_a, b_a, w_x, b_x, lam, norm_rg, norm_attn, w_out, norm_ffn,
           w_ffn_gate, w_ffn_up, w_ffn_down):
    depth = w_in.shape[0]
    assert depth == 1, "single-layer step"
    b, t, d = x_prompt.shape
    bs, ts, _ = x_sample.shape
    nh, hd = cache_k.shape[3], cache_k.shape[4]
    att = nh * hd
    rg = state_h.shape[-1]
    n_pool, page = cache_k.shape[1], cache_k.shape[2]
    in_cols = w_in.shape[2]
    assert in_cols == 2 * rg + 3 * att + nh

    tm_in = 512
    tm_rg = 512
    tq = 512
    tk = 512
    tm_ffn = 512
    nchain = 2
    npg = min(32, page_table.shape[1] // 2)
    assert page_table.shape[1] % (2 * npg) == 0 and npg % nchain == 0

    wcols = -(-in_cols // LANES) * LANES
    w_in_b = jnp.pad(w_in[0], ((0, 0), (0, wcols - in_cols))).astype(BF16)
    ones_heads = jnp.kron(jnp.eye(nh, dtype=F32), jnp.ones((hd, hd), F32)).astype(BF16)
    gq_row = jnp.tile(g_q[0], nh)[None, :] * (hd ** -0.5)
    gq_row2 = gq_row * LOG2E
    gk_row = jnp.tile(g_k[0], nh)[None, :]
    assert 2 * hd == LANES and 3 * nh <= LANES
    sel_np = [[0.0] * att for _ in range(LANES)]
    for h in range(nh):
        for kk in range(3):
            sel_np[kk * nh + h][LANES * (h // 2) + 3 * (h % 2) + kk] = 1.0
    sel_terms = jnp.array(sel_np, BF16)
    wa_pairs = _pair_blockdiag(w_a[0], w_x[0])
    wo_b = w_out[0].astype(BF16)
    wg_b = w_ffn_gate[0].astype(BF16)
    wu_b = w_ffn_up[0].astype(BF16)
    wd_b = w_ffn_down[0].astype(BF16)
    row = lambda a: a[0][None, :]

    mods = _ada(jnp.concatenate([c_prompt, c_sample], axis=0), w_ada[0], b_ada[0][None, :])
    mods = mods.reshape(b + bs, 6, d)
    mods_p = [mods[:b, i:i + 1] for i in range(6)]
    mods_s = [jnp.repeat(mods[b:, i], ts, axis=0)[None] for i in range(6)]

    xp2d = x_prompt.reshape(b * t, d)
    (xr, gr, knT, ka, vT32, qT, vT, logf) = _inproj_prompt(
        xp2d, mods_p[0], mods_p[1], row(norm_mix), w_in_b, ones_heads, gq_row2, gk_row, row(b_f), sel_terms,
        b=b, t=t, tm=tm_in, rg=rg, att=att, hd=hd, nh=nh)
    rgn, h_last = _rg_prompt(xr, gr, w_conv[0], row(b_conv), wa_pairs, row(b_a), row(b_x), row(lam),
                             row(norm_rg), b=b, t=t, tm=tm_rg)
    attn = _attn_prompt(qT, ka.reshape(b, t, 2 * att), vT, tq=tq, tk=tk, hd=hd, npair=2, nsplit=1, nunroll=4)
    y_prompt = _merge_ffn(xp2d, rgn, attn.reshape(b * t, att), [mods_p[2], mods_p[3], mods_p[4], mods_p[5]],
                          row(norm_attn), row(norm_ffn), wo_b, wg_b, wu_b, wd_b,
                          tm=tm_ffn, rows_per_group=t, mod_rows=1)

    ns = bs * ts
    xs2d = x_sample.reshape(ns, d)
    (xr_s, gr_s, q_s, kn_s, v_s, logf_s) = _inproj_sample(
        xs2d, mods_s[0], mods_s[1], row(norm_mix), w_in_b, ones_heads, gq_row, gk_row, row(b_f),
        rg=rg, att=att, hd=hd, nh=nh)
    x_ext = jnp.concatenate([state_conv[0], xr_s.reshape(bs, ts, rg)], axis=1)
    xwin = [x_ext[:, j:j + ts].reshape(ns, rg) for j in range(CONV_WIDTH)]
    h0_rows = jnp.repeat(state_h[0], ts, axis=0)
    rgn_s, h_rows = _rg_sample(xwin, gr_s, h0_rows, w_conv[0], row(b_conv), wa_pairs, row(b_a), row(b_x),
                               row(lam), row(norm_rg), seg=ts)
    tpad = SUBLANES
    pad_t = lambda a: jnp.pad(a, ((0, 0), (0, tpad - ts), (0, 0)))
    ck_t = jnp.transpose(cache_k[0], (0, 2, 3, 1)).reshape(n_pool, att, page)
    cv_t = jnp.transpose(cache_v[0], (0, 2, 3, 1)).reshape(n_pool, att, page)
    clf_t = jnp.transpose(cache_logf[0], (0, 2, 1))
    lfs_t = jnp.pad(jnp.transpose(logf_s.reshape(bs, ts, nh), (0, 2, 1)), ((0, 0), (0, 0), (0, LANES - ts)))
    attn_s = _attn_sample(
        page_table, ck_t, cv_t, clf_t, q_s.reshape(bs, ts, att), pad_t(kn_s.reshape(bs, ts, att)),
        pad_t(v_s.reshape(bs, ts, att)), lfs_t, npg=npg, nchain=nchain, hd=hd, nh=nh)
    y_sample = _merge_ffn(xs2d, rgn_s, attn_s.reshape(ns, att), [mods_s[2], mods_s[3], mods_s[4], mods_s[5]],
                          row(norm_attn), row(norm_ffn), wo_b, wg_b, wu_b, wd_b,
                          tm=min(tm_ffn, ns), rows_per_group=ns, mod_rows=min(tm_ffn, ns))

    nc = CONV_WIDTH - 1
    heads_last = lambda a: jnp.transpose(a.reshape(b, nh, hd, t), (0, 3, 1, 2))[None]
    return (y_prompt.reshape(b, t, d), y_sample.reshape(bs, ts, d),
            heads_last(knT), heads_last(vT32), logf.reshape(1, b, t, nh),
            xr.reshape(b, t, rg)[:, t - nc:, :][None], h_last.reshape(1, b, rg),
            kn_s.reshape(1, bs, ts, nh, hd), v_s.reshape(1, bs, ts, nh, hd), logf_s.reshape(1, bs, ts, nh),
            x_ext[:, ts:, :][None], h_rows.reshape(bs, ts, rg)[:, ts - 1, :][None])
```
